```python
import jax, jax.numpy as jnp
from jax import lax
import numpy as np

D_MODEL = 2048
BATCH = 4
SEQ = 2048
DEPTH = 2

N_HEADS = 16
HEAD_DIM = D_MODEL // N_HEADS
D_ATTN = N_HEADS * HEAD_DIM
D_FF = ((8 * D_MODEL // 3 + 255) // 256) * 256
N_EXPERTS = 8
TOP_K = 2
D_FF_EXPERT = 7 * D_MODEL // 2
Q_BLOCK = 128
RMS_EPS = 1e-6
N_A_LAYERS = DEPTH // 2
N_B_LAYERS = DEPTH - N_A_LAYERS
N_DENSE = (DEPTH + 1) // 2
N_MOE = DEPTH // 2

kernel_name = "yoco_stickbreak_fox_moe_hybrid"


def rms_norm(x, g):
    xf = x.astype(jnp.float32)
    y = xf * lax.rsqrt(jnp.mean(xf * xf, axis=-1, keepdims=True) + RMS_EPS)
    return (y * g.astype(jnp.float32)).astype(x.dtype)


def split_heads(t):
    b, s, _ = t.shape
    return t.reshape(b, s, N_HEADS, HEAD_DIM).transpose(0, 2, 1, 3)


def merge_heads(t):
    b, h, s, d = t.shape
    return t.transpose(0, 2, 1, 3).reshape(b, s, h * d)


def stick_breaking_attention(q, k, v):
    seq = q.shape[2]
    scale = HEAD_DIM ** -0.5
    outs = []
    for i in range(seq // Q_BLOCK):
        q0, end = i * Q_BLOCK, (i + 1) * Q_BLOCK
        qb = q[:, :, q0:end]
        kb = k[:, :, :end]
        vb = v[:, :, :end]
        z = jnp.einsum('bhqd,bhkd->bhqk', qb, kb).astype(jnp.float32) * scale
        qpos = q0 + jnp.arange(Q_BLOCK)
        kpos = jnp.arange(end)
        mask = kpos[None, :] < qpos[:, None]
        log_1m = jnp.where(mask, jax.nn.log_sigmoid(-z), 0.0)
        suffix = lax.cumsum(log_1m, axis=3, reverse=True) - log_1m
        w = jnp.where(mask, jnp.exp(jax.nn.log_sigmoid(z) + suffix), 0.0)
        outs.append(jnp.einsum('bhqk,bhkd->bhqd', w.astype(vb.dtype), vb))
    return jnp.concatenate(outs, axis=2)


def forgetting_attention(q, k, v, cum_log_f):
    seq = q.shape[2]
    scale = HEAD_DIM ** -0.5
    outs = []
    for i in range(seq // Q_BLOCK):
        q0, end = i * Q_BLOCK, (i + 1) * Q_BLOCK
        qb = q[:, :, q0:end]
        kb = k[:, :, :end]
        vb = v[:, :, :end]
        z = jnp.einsum('bhqd,bhkd->bhqk', qb, kb).astype(jnp.float32) * scale
        z = z + cum_log_f[:, :, q0:end, None] - cum_log_f[:, :, None, :end]
        qpos = q0 + jnp.arange(Q_BLOCK)
        kpos = jnp.arange(end)
        mask = kpos[None, :] <= qpos[:, None]
        p = jax.nn.softmax(jnp.where(mask, z, -jnp.inf), axis=-1)
        outs.append(jnp.einsum('bhqk,bhkd->bhqd', p.astype(vb.dtype), vb))
    return jnp.concatenate(outs, axis=2)


def swiglu(x, w_gu, w_down):
    gu = x @ w_gu
    g, u = jnp.split(gu, 2, axis=-1)
    return (jax.nn.silu(g) * u) @ w_down


def moe_swiglu(x, w_router, w_gu, w_down):
    b, s, d = x.shape
    xt = x.reshape(b * s, d)
    logits = (xt @ w_router).astype(jnp.float32)
    top_vals, top_idx = lax.top_k(logits, TOP_K)
    top_w = jax.nn.softmax(top_vals, axis=-1)
    gates = jnp.sum(jax.nn.one_hot(top_idx, N_EXPERTS, dtype=jnp.float32)
                    * top_w[..., None], axis=1)
    y = jnp.zeros((b * s, d), jnp.float32)
    for e in range(N_EXPERTS):
        y = y + gates[:, e:e + 1] * swiglu(xt, w_gu[e], w_down[e]).astype(jnp.float32)
    return y.astype(x.dtype).reshape(b, s, d)


def setup_inputs(seed: int = 0) -> dict:
    key = jax.random.key(seed)
    ks = jax.random.split(key, 16)
    f32 = jnp.float32

    def nrm(k, shape, fan_in):
        return jax.random.normal(k, shape, f32) * (fan_in ** -0.5)

    def gain(k, shape):
        return 1.0 + 0.02 * jax.random.normal(k, shape, f32)

    return {
        "x": jax.random.normal(ks[0], (BATCH, SEQ, D_MODEL), f32),
        "norm_mix": gain(ks[1], (DEPTH, D_MODEL)),
        "norm_ffn": gain(ks[2], (DEPTH, D_MODEL)),
        "w_qkv_a": nrm(ks[3], (N_A_LAYERS, D_MODEL, 3 * D_ATTN), D_MODEL),
        "w_o_a": nrm(ks[4], (N_A_LAYERS, D_ATTN, D_MODEL), D_ATTN),
        "norm_kv": gain(ks[5], (D_MODEL,)),
        "w_kvf_b": nrm(ks[6], (D_MODEL, 2 * D_ATTN + N_HEADS), D_MODEL),
        "b_f": jax.random.uniform(ks[7], (N_HEADS,), f32, 1.0, 4.0),
        "w_q_b": nrm(ks[8], (N_B_LAYERS, D_MODEL, D_ATTN), D_MODEL),
        "w_o_b": nrm(ks[9], (N_B_LAYERS, D_ATTN, D_MODEL), D_ATTN),
        "w_gu_dense": nrm(ks[10], (N_DENSE, D_MODEL, 2 * D_FF), D_MODEL),
        "w_down_dense": nrm(ks[11], (N_DENSE, D_FF, D_MODEL), D_FF),
        "w_router": nrm(ks[12], (N_MOE, D_MODEL, N_EXPERTS), D_MODEL),
        "w_gu_exp": nrm(ks[13], (N_MOE, N_EXPERTS, D_MODEL, 2 * D_FF_EXPERT), D_MODEL),
        "w_down_exp": nrm(ks[14], (N_MOE, N_EXPERTS, D_FF_EXPERT, D_MODEL), D_FF_EXPERT),
        "norm_final": gain(ks[15], (D_MODEL,)),
    }


def reference(x, norm_mix, norm_ffn, w_qkv_a, w_o_a, norm_kv, w_kvf_b, b_f, w_q_b, w_o_b,
              w_gu_dense, w_down_dense, w_router, w_gu_exp, w_down_exp, norm_final):
    h = x
    k_s = v_s = cum_log_f = None
    for layer in range(DEPTH):
        hn = rms_norm(h, norm_mix[layer])
        if layer < N_A_LAYERS:
            q, k, v = jnp.split(hn @ w_qkv_a[layer], 3, axis=-1)
            o = stick_breaking_attention(split_heads(q), split_heads(k), split_heads(v))
            h = h + merge_heads(o) @ w_o_a[layer]
        else:
            if layer == N_A_LAYERS:
                kvf = rms_norm(h, norm_kv) @ w_kvf_b
                k_s = split_heads(kvf[..., :D_ATTN])
                v_s = split_heads(kvf[..., D_ATTN:2 * D_ATTN])
                f_logit = kvf[..., 2 * D_ATTN:].astype(jnp.float32) + b_f.astype(jnp.float32)
                log_f = jax.nn.log_sigmoid(f_logit)
                cum_log_f = jnp.cumsum(log_f, axis=1).transpose(0, 2, 1)
            j = layer - N_A_LAYERS
            q = split_heads(hn @ w_q_b[j])
            o = forgetting_attention(q, k_s, v_s, cum_log_f)
            h = h + merge_heads(o) @ w_o_b[j]
        hn = rms_norm(h, norm_ffn[layer])
        if layer % 2 == 0:
            h = h + swiglu(hn, w_gu_dense[layer // 2], w_down_dense[layer // 2])
        else:
            i = layer // 2
            h = h + moe_swiglu(hn, w_router[i], w_gu_exp[i], w_down_exp[i])
    return rms_norm(h, norm_final)
```

```python
import functools

import jax
import jax.numpy as jnp
from jax import lax
from jax.experimental import pallas as pl
from jax.experimental.pallas import tpu as pltpu

RMS_EPS = 1e-6
TOP_K = 2

V7X_VMEM_LIMIT_BYTES = 56 * 1024 * 1024

BF16 = jnp.bfloat16
F32 = jnp.float32


def _params(*semantics):
    return pltpu.CompilerParams(
        dimension_semantics=semantics, vmem_limit_bytes=V7X_VMEM_LIMIT_BYTES)


def _dot(a, b):
    return jnp.dot(a, b, preferred_element_type=F32)


def _dot_nt(a, b):
    return lax.dot_general(a, b, (((1,), (1,)), ((), ())), preferred_element_type=F32)


def _split3(x):
    hi = x.astype(BF16)
    r1 = x - hi.astype(F32)
    mid = r1.astype(BF16)
    lo = (r1 - mid.astype(F32)).astype(BF16)
    return hi, mid, lo


def _log_sigmoid(z):
    return jnp.minimum(z, 0.0) - jnp.log1p(jnp.exp(-jnp.abs(z)))


def _rmsnorm_kernel(*refs, n_gains, has_proj):
    x_ref = refs[0]
    g_refs = refs[1:1 + n_gains]
    pos = 1 + n_gains
    p_ref = refs[pos] if has_proj else None
    pos += int(has_proj)
    o_refs = refs[pos:pos + n_gains]
    po_ref = refs[pos + n_gains] if has_proj else None

    x = x_ref[...]
    y = x * lax.rsqrt(jnp.mean(x * x, axis=-1, keepdims=True) + RMS_EPS)
    for g_ref, o_ref in zip(g_refs, o_refs):
        o_ref[...] = (y * g_ref[...]).astype(o_ref.dtype)
    if has_proj:
        yn = y * g_refs[-1][...]
        yh, ym, _ = _split3(yn)
        ph, pm, _ = _split3(p_ref[...])
        po_ref[...] = _dot_nt(ph, yh) + (_dot_nt(ph, ym) + _dot_nt(pm, yh))


def _rmsnorm(x, gains, out_dtype, proj_t=None, tm=256):
    t, d = x.shape
    tm = min(tm, t)
    n_gains = len(gains)
    has_proj = proj_t is not None
    in_specs = [pl.BlockSpec((tm, d), lambda i: (i, 0))]
    in_specs += [pl.BlockSpec((1, d), lambda i: (0, 0))] * n_gains
    args = [x] + [g.reshape(1, d).astype(F32) for g in gains]
    out_shape = [jax.ShapeDtypeStruct((t, d), out_dtype)] * n_gains
    out_specs = [pl.BlockSpec((tm, d), lambda i: (i, 0))] * n_gains
    if has_proj:
        n = proj_t.shape[0]
        in_specs.append(pl.BlockSpec((n, d), lambda i: (0, 0)))
        args.append(proj_t)
        out_shape.append(jax.ShapeDtypeStruct((n, t), F32))
        out_specs.append(pl.BlockSpec((n, tm), lambda i: (0, i)))
    outs = pl.pallas_call(
        functools.partial(_rmsnorm_kernel, n_gains=n_gains, has_proj=has_proj),
        grid=(t // tm,),
        in_specs=in_specs,
        out_specs=out_specs,
        out_shape=out_shape,
        compiler_params=_params("parallel"),
        name="rmsnorm",
    )(*args)
    return outs


def _mm_plain_kernel(a_ref, w_ref, o_ref):
    o_ref[...] = _dot(a_ref[...], w_ref[...].astype(BF16)).astype(o_ref.dtype)


def _mm_res_kernel(a_ref, w_ref, r_ref, o_ref):
    o_ref[...] = r_ref[...] + _dot(a_ref[...], w_ref[...].astype(BF16))


def _mm_swiglu_kernel(a_ref, wg_ref, wu_ref, o_ref):
    a = a_ref[...]
    g = _dot(a, wg_ref[...].astype(BF16))
    u = _dot(a, wu_ref[...].astype(BF16))
    o_ref[...] = (g * jax.nn.sigmoid(g) * u).astype(o_ref.dtype)


def _w_spec(w, k, tn, prefix, col_block_offset=0):
    lead = (None,) * len(prefix)
    return pl.BlockSpec(
        lead + (k, tn), lambda n, m: tuple(prefix) + (0, n + col_block_offset))


def _matmul(a, w, prefix, n_out, out_dtype, residual=None, tm=1024, tn=1024, name="mm"):
    m_dim, k = a.shape
    tm, tn = min(tm, m_dim), min(tn, n_out)
    grid = (n_out // tn, m_dim // tm)
    in_specs = [pl.BlockSpec((tm, k), lambda n, m: (m, 0)), _w_spec(w, k, tn, prefix)]
    args = [a, w]
    kernel = _mm_plain_kernel
    if residual is not None:
        in_specs.append(pl.BlockSpec((tm, tn), lambda n, m: (m, n)))
        args.append(residual)
        kernel = _mm_res_kernel
    return pl.pallas_call(
        kernel,
        grid=grid,
        in_specs=in_specs,
        out_specs=pl.BlockSpec((tm, tn), lambda n, m: (m, n)),
        out_shape=jax.ShapeDtypeStruct((m_dim, n_out), out_dtype),
        compiler_params=_params("parallel", "parallel"),
        name=name,
    )(*args)


def _matmul_swiglu(a, w_gu, prefix, tm=1024, tn=512, name="mm_swiglu"):
    m_dim, k = a.shape
    f = w_gu.shape[-1] // 2
    tm, tn = min(tm, m_dim), min(tn, f)
    grid = (f // tn, m_dim // tm)
    return pl.pallas_call(
        _mm_swiglu_kernel,
        grid=grid,
        in_specs=[
            pl.BlockSpec((tm, k), lambda n, m: (m, 0)),
            _w_spec(w_gu, k, tn, prefix),
            _w_spec(w_gu, k, tn, prefix, col_block_offset=f // tn),
        ],
        out_specs=pl.BlockSpec((tm, tn), lambda n, m: (m, n)),
        out_shape=jax.ShapeDtypeStruct((m_dim, f), BF16),
        compiler_params=_params("parallel", "parallel"),
        name=name,
    )(a, w_gu, w_gu)


def _mm_down_kernel(*refs, nk, gate_col):
    if gate_col is None:
        a_ref, w_ref, r_ref, o_ref, acc_ref = refs
        g_ref = None
    else:
        a_ref, w_ref, r_ref, g_ref, o_ref, acc_ref = refs
    k = pl.program_id(1)

    @pl.when(k == 0)
    def _():
        acc_ref[...] = jnp.zeros_like(acc_ref)

    acc_ref[...] += _dot(a_ref[...], w_ref[...].astype(BF16))

    @pl.when(k == nk - 1)
    def _():
        acc = acc_ref[...]
        if g_ref is not None:
            acc = g_ref[:, gate_col:gate_col + 1] * acc
        o_ref[...] = r_ref[...] + acc


def _matmul_down(a, w, prefix, residual, gates=None, gate_col=None, tm=1024, tk=512,
                 name="mm_down"):
    m_dim, k_dim = a.shape
    n = w.shape[-1]
    tm, tk = min(tm, m_dim), min(tk, k_dim)
    nk = k_dim // tk
    lead = (None,) * len(prefix)
    in_specs = [
        pl.BlockSpec((tm, tk), lambda m, k: (m, k)),
        pl.BlockSpec(lead + (tk, n), lambda m, k: tuple(prefix) + (k, 0)),
        pl.BlockSpec((tm, n), lambda m, k: (m, 0)),
    ]
    args = [a, w, residual]
    if gates is not None:
        in_specs.append(pl.BlockSpec((tm, gates.shape[1]), lambda m, k: (m, 0)))
        args.append(gates)
    return pl.pallas_call(
        functools.partial(_mm_down_kernel, nk=nk, gate_col=gate_col),
        grid=(m_dim // tm, nk),
        in_specs=in_specs,
        out_specs=pl.BlockSpec((tm, n), lambda m, k: (m, 0)),
        out_shape=jax.ShapeDtypeStruct((m_dim, n), F32),
        scratch_shapes=[pltpu.VMEM((tm, n), F32)],
        compiler_params=_params("parallel", "arbitrary"),
        name=name,
    )(*args)


def _strict_lower_ones(t):
    j = lax.broadcasted_iota(jnp.int32, (t, t), 0)
    s = lax.broadcasted_iota(jnp.int32, (t, t), 1)
    return jnp.where(j > s, 1.0, 0.0).astype(BF16)


def _sb_attn_kernel(q_ref, k_ref, v_ref, o_ref, acc_ref, run_ref, *, tq, scale):
    i = pl.program_id(2)
    q = q_ref[...]
    tri = _strict_lower_ones(tq)
    row = lax.broadcasted_iota(jnp.int32, (tq, tq), 0)
    col = lax.broadcasted_iota(jnp.int32, (tq, tq), 1)
    diag_mask = col < row

    def block(j, mask):
        start = pl.multiple_of(j * tq, tq)
        kb = k_ref[pl.ds(start, tq), :]
        vb = v_ref[pl.ds(start, tq), :]
        z = _dot_nt(q, kb) * scale
        ls = _log_sigmoid(z)
        l1m = ls - z
        if mask is not None:
            l1m = jnp.where(mask, l1m, 0.0)
        hi = l1m.astype(BF16)
        lo = (l1m - hi.astype(F32)).astype(BF16)
        suffix = _dot(hi, tri) + _dot(lo, tri)
        w = jnp.exp(ls + suffix + run_ref[...])
        if mask is not None:
            w = jnp.where(mask, w, 0.0)
        acc_ref[...] += _dot(w.astype(BF16), vb)
        run_ref[...] += jnp.sum(l1m, axis=-1, keepdims=True)

    acc_ref[...] = jnp.zeros_like(acc_ref)
    run_ref[...] = jnp.zeros_like(run_ref)
    block(i, diag_mask)

    def body(t, carry):
        block(i - 1 - t, None)
        return carry

    lax.fori_loop(0, i, body, 0)
    o_ref[...] = acc_ref[...].astype(o_ref.dtype)


def _sb_attention(qkv, batch, seq, n_heads, head_dim, tq=256):
    nq = seq // tq
    return pl.pallas_call(
        functools.partial(_sb_attn_kernel, tq=tq, scale=head_dim ** -0.5),
        grid=(batch, n_heads, nq),
        in_specs=[
            pl.BlockSpec((tq, head_dim), lambda b, h, i: (b * nq + i, h)),
            pl.BlockSpec((seq, head_dim), lambda b, h, i: (b, n_heads + h)),
            pl.BlockSpec((seq, head_dim), lambda b, h, i: (b, 2 * n_heads + h)),
        ],
        out_specs=pl.BlockSpec((tq, head_dim), lambda b, h, i: (b * nq + i, h)),
        out_shape=jax.ShapeDtypeStruct((batch * seq, n_heads * head_dim), BF16),
        scratch_shapes=[pltpu.VMEM((tq, head_dim), F32), pltpu.VMEM((tq, 1), F32)],
        compiler_params=_params("parallel", "parallel", "parallel"),
        name="sb_attention",
    )(qkv, qkv, qkv)


def _fgate_cumsum_kernel(fl_ref, bf_ref, c_ref, *, seq):
    x = _log_sigmoid(fl_ref[...] + bf_ref[...])
    j = lax.broadcasted_iota(jnp.int32, (seq, seq), 0)
    t = lax.broadcasted_iota(jnp.int32, (seq, seq), 1)
    upper = jnp.where(j <= t, 1.0, 0.0).astype(BF16)
    hi, mid, lo = _split3(x)
    c_ref[...] = _dot(hi, upper) + (_dot(mid, upper) + _dot(lo, upper))


def _fgate_cumsum(flogit_t, b_f, batch, seq):
    n_heads = flogit_t.shape[0]
    return pl.pallas_call(
        functools.partial(_fgate_cumsum_kernel, seq=seq),
        grid=(batch,),
        in_specs=[
            pl.BlockSpec((n_heads, seq), lambda b: (0, b)),
            pl.BlockSpec((n_heads, 1), lambda b: (0, 0)),
        ],
        out_specs=pl.BlockSpec((None, n_heads, seq), lambda b: (b, 0, 0)),
        out_shape=jax.ShapeDtypeStruct((batch, n_heads, seq), F32),
        compiler_params=_params("parallel"),
        name="fgate_cumsum",
    )(flogit_t, b_f.reshape(n_heads, 1).astype(F32))


def _fox_attn_kernel(q_ref, k_ref, v_ref, cq_ref, ck_ref, o_ref, acc_ref, m_ref, l_ref,
                     *, tq, scale):
    i = pl.program_id(2)
    q = q_ref[...]
    cq = cq_ref[...]
    row = lax.broadcasted_iota(jnp.int32, (tq, tq), 0)
    col = lax.broadcasted_iota(jnp.int32, (tq, tq), 1)
    diag_mask = col <= row

    def block(j, mask, first):
        start = pl.multiple_of(j * tq, tq)
        kb = k_ref[pl.ds(start, tq), :]
        vb = v_ref[pl.ds(start, tq), :]
        s = _dot_nt(q, kb) * scale + cq - ck_ref[:, pl.ds(start, tq)]
        if mask is not None:
            s = jnp.where(mask, s, -jnp.inf)
        blk_max = jnp.max(s, axis=-1, keepdims=True)
        if first:
            m_new = blk_max
            p = jnp.exp(s - m_new)
            l_ref[...] = jnp.sum(p, axis=-1, keepdims=True)
            acc_ref[...] = _dot(p.astype(BF16), vb)
        else:
            m_old = m_ref[...]
            m_new = jnp.maximum(m_old, blk_max)
            alpha = jnp.exp(m_old - m_new)
            p = jnp.exp(s - m_new)
            l_ref[...] = alpha * l_ref[...] + jnp.sum(p, axis=-1, keepdims=True)
            acc_ref[...] = alpha * acc_ref[...] + _dot(p.astype(BF16), vb)
        m_ref[...] = m_new

    block(i, diag_mask, True)

    def body(t, carry):
        block(i - 1 - t, None, False)
        return carry

    lax.fori_loop(0, i, body, 0)
    o_ref[...] = (acc_ref[...] / l_ref[...]).astype(o_ref.dtype)


def _fox_attention(q, kv, cum_log_f, batch, seq, n_heads, head_dim, tq=256):
    nq = seq // tq
    cq = cum_log_f.reshape(batch, n_heads, seq, 1)
    ck = cum_log_f.reshape(batch, n_heads, 1, seq)
    return pl.pallas_call(
        functools.partial(_fox_attn_kernel, tq=tq, scale=head_dim ** -0.5),
        grid=(batch, n_heads, nq),
        in_specs=[
            pl.BlockSpec((tq, head_dim), lambda b, h, i: (b * nq + i, h)),
            pl.BlockSpec((seq, head_dim), lambda b, h, i: (b, h)),
            pl.BlockSpec((seq, head_dim), lambda b, h, i: (b, n_heads + h)),
            pl.BlockSpec((None, None, tq, 1), lambda b, h, i: (b, h, i, 0)),
            pl.BlockSpec((None, None, 1, seq), lambda b, h, i: (b, h, 0, 0)),
        ],
        out_specs=pl.BlockSpec((tq, head_dim), lambda b, h, i: (b * nq + i, h)),
        out_shape=jax.ShapeDtypeStruct((batch * seq, n_heads * head_dim), BF16),
        scratch_shapes=[
            pltpu.VMEM((tq, head_dim), F32),
            pltpu.VMEM((tq, 1), F32),
            pltpu.VMEM((tq, 1), F32),
        ],
        compiler_params=_params("parallel", "parallel", "parallel"),
        name="fox_attention",
    )(q, kv, kv, cq, ck)


def _router_gates_kernel(l_ref, g_ref):
    l = l_ref[...]
    n_exp = l.shape[0]
    e_idx = lax.broadcasted_iota(jnp.int32, l.shape, 0)
    m1 = jnp.max(l, axis=0, keepdims=True)
    i1 = jnp.min(jnp.where(l == m1, e_idx, n_exp), axis=0, keepdims=True)
    sel1 = e_idx == i1
    l2 = jnp.where(sel1, -jnp.inf, l)
    m2 = jnp.max(l2, axis=0, keepdims=True)
    i2 = jnp.min(jnp.where(l2 == m2, e_idx, n_exp), axis=0, keepdims=True)
    sel2 = e_idx == i2
    e2 = jnp.exp(m2 - m1)
    denom = 1.0 + e2
    g_ref[...] = jnp.where(sel1, 1.0 / denom, jnp.where(sel2, e2 / denom, 0.0))


def _router_gates(logits_t, tm=2048):
    n_exp, t = logits_t.shape
    tm = min(tm, t)
    return pl.pallas_call(
        _router_gates_kernel,
        grid=(t // tm,),
        in_specs=[pl.BlockSpec((n_exp, tm), lambda i: (0, i))],
        out_specs=pl.BlockSpec((n_exp, tm), lambda i: (0, i)),
        out_shape=jax.ShapeDtypeStruct((n_exp, t), F32),
        compiler_params=_params("parallel"),
        name="router_gates",
    )(logits_t)


def kernel(x, norm_mix, norm_ffn, w_qkv_a, w_o_a, norm_kv, w_kvf_b, b_f, w_q_b, w_o_b,
           w_gu_dense, w_down_dense, w_router, w_gu_exp, w_down_exp, norm_final):
    batch, seq, d_model = x.shape
    n_heads = b_f.shape[0]
    d_attn = w_o_a.shape[1]
    head_dim = d_attn // n_heads
    n_experts = w_router.shape[-1]
    assert TOP_K == 2 and norm_mix.shape[0] == 2, "kernel is written for the depth-2 trunk"
    t = batch * seq
    h = x.reshape(t, d_model)

    (hn,) = _rmsnorm(h, [norm_mix[0]], BF16)
    qkv = _matmul(hn, w_qkv_a, (0,), 3 * d_attn, BF16, name="mm_qkv_a")
    o = _sb_attention(qkv, batch, seq, n_heads, head_dim)
    h = _matmul(o, w_o_a, (0,), d_model, F32, residual=h, name="mm_o_a")
    (hn,) = _rmsnorm(h, [norm_ffn[0]], BF16)
    act = _matmul_swiglu(hn, w_gu_dense, (0,), name="mm_gu_dense")
    h = _matmul_down(act, w_down_dense, (0,), h, name="mm_down_dense")

    w_f_t = w_kvf_b[:, 2 * d_attn:].T
    hn_q, hn_kv, flogit_t = _rmsnorm(h, [norm_mix[1], norm_kv], BF16, proj_t=w_f_t)
    kv = _matmul(hn_kv, w_kvf_b, (), 2 * d_attn, BF16, name="mm_kv_b")
    q = _matmul(hn_q, w_q_b, (0,), d_attn, BF16, name="mm_q_b")
    cum_log_f = _fgate_cumsum(flogit_t, b_f, batch, seq)
    o = _fox_attention(q, kv, cum_log_f, batch, seq, n_heads, head_dim)
    h = _matmul(o, w_o_b, (0,), d_model, F32, residual=h, name="mm_o_b")

    hn, logits_t = _rmsnorm(h, [norm_ffn[1]], BF16, proj_t=w_router[0].T)
    gates = _router_gates(logits_t).T
    for e in range(n_experts):
        act = _matmul_swiglu(hn, w_gu_exp, (0, e), name="mm_gu_exp")
        h = _matmul_down(act, w_down_exp, (0, e), h, gates=gates, gate_col=e,
                         name="mm_down_exp")

    (out,) = _rmsnorm(h, [norm_final], x.dtype)
    return out.reshape(batch, seq, d_model)
```

```python
import functools

import jax
import jax.numpy as jnp
from jax import lax
from jax.experimental import pallas as pl
from jax.experimental.pallas import tpu as pltpu

RMS_EPS = 1e-6
TOP_K = 2

V7X_VMEM_LIMIT_BYTES = 56 * 1024 * 1024

VREG_SUBLANES = 8
VREG_LANES = 128

MOE_ROW_TILE = 512
ATTN_HEADS_PER_STEP = 1
ATTN_TQ = 512
ATTN_TK = 256

BF16 = jnp.bfloat16
F32 = jnp.float32


def _params(*semantics):
    return pltpu.CompilerParams(
        dimension_semantics=semantics, vmem_limit_bytes=V7X_VMEM_LIMIT_BYTES)


def _dot(a, b):
    return jnp.dot(a, b, preferred_element_type=F32)


def _dot_nt(a, b):
    return lax.dot_general(a, b, (((1,), (1,)), ((), ())), preferred_element_type=F32)


def _split3(x):
    hi = x.astype(BF16)
    r1 = x - hi.astype(F32)
    mid = r1.astype(BF16)
    lo = (r1 - mid.astype(F32)).astype(BF16)
    return hi, mid, lo


def _log_sigmoid(z):
    return jnp.minimum(z, 0.0) - jnp.log(1.0 + jnp.exp(-jnp.abs(z)))


def _rmsnorm_kernel(*refs, n_gains, has_proj):
    x_ref = refs[0]
    g_refs = refs[1:1 + n_gains]
    pos = 1 + n_gains
    p_ref = refs[pos] if has_proj else None
    pos += int(has_proj)
    o_refs = refs[pos:pos + n_gains]
    po_ref = refs[pos + n_gains] if has_proj else None

    x = x_ref[...]
    y = x * lax.rsqrt(jnp.mean(x * x, axis=-1, keepdims=True) + RMS_EPS)
    for g_ref, o_ref in zip(g_refs, o_refs):
        o_ref[...] = (y * g_ref[...]).astype(o_ref.dtype)
    if has_proj:
        yn = y * g_refs[-1][...]
        yh, ym, _ = _split3(yn)
        ph, pm, _ = _split3(p_ref[...])
        po_ref[...] = _dot_nt(ph, yh) + (_dot_nt(ph, ym) + _dot_nt(pm, yh))


def _rmsnorm(x, gains, out_dtype, proj_t=None, tm=256):
    t, d = x.shape
    tm = min(tm, t)
    n_gains = len(gains)
    has_proj = proj_t is not None
    in_specs = [pl.BlockSpec((tm, d), lambda i: (i, 0))]
    in_specs += [pl.BlockSpec((1, d), lambda i: (0, 0))] * n_gains
    args = [x] + [g.reshape(1, d).astype(F32) for g in gains]
    out_shape = [jax.ShapeDtypeStruct((t, d), out_dtype)] * n_gains
    out_specs = [pl.BlockSpec((tm, d), lambda i: (i, 0))] * n_gains
    if has_proj:
        n = proj_t.shape[0]
        in_specs.append(pl.BlockSpec((n, d), lambda i: (0, 0)))
        args.append(proj_t)
        out_shape.append(jax.ShapeDtypeStruct((n, t), F32))
        out_specs.append(pl.BlockSpec((n, tm), lambda i: (0, i)))
    outs = pl.pallas_call(
        functools.partial(_rmsnorm_kernel, n_gains=n_gains, has_proj=has_proj),
        grid=(t // tm,),
        in_specs=in_specs,
        out_specs=out_specs,
        out_shape=out_shape,
        compiler_params=_params("parallel"),
        name="rmsnorm",
    )(*args)
    return outs


def _mm_plain_kernel(a_ref, w_ref, o_ref):
    o_ref[...] = _dot(a_ref[...], w_ref[...].astype(BF16)).astype(o_ref.dtype)


def _mm_res_kernel(a_ref, w_ref, r_ref, o_ref):
    o_ref[...] = r_ref[...] + _dot(a_ref[...], w_ref[...].astype(BF16))


def _mm_swiglu_kernel(a_ref, wg_ref, wu_ref, o_ref):
    a = a_ref[...]
    g = _dot(a, wg_ref[...].astype(BF16))
    u = _dot(a, wu_ref[...].astype(BF16))
    o_ref[...] = (g * jax.nn.sigmoid(g) * u).astype(o_ref.dtype)


def _w_spec(w, k, tn, prefix, col_block_offset=0):
    lead = (None,) * len(prefix)
    return pl.BlockSpec(
        lead + (k, tn), lambda n, m: tuple(prefix) + (0, n + col_block_offset))


def _matmul(a, w, prefix, n_out, out_dtype, residual=None, tm=1024, tn=1024, name="mm"):
    m_dim, k = a.shape
    tm, tn = min(tm, m_dim), min(tn, n_out)
    grid = (n_out // tn, m_dim // tm)
    in_specs = [pl.BlockSpec((tm, k), lambda n, m: (m, 0)), _w_spec(w, k, tn, prefix)]
    args = [a, w]
    kernel = _mm_plain_kernel
    if residual is not None:
        in_specs.append(pl.BlockSpec((tm, tn), lambda n, m: (m, n)))
        args.append(residual)
        kernel = _mm_res_kernel
    return pl.pallas_call(
        kernel,
        grid=grid,
        in_specs=in_specs,
        out_specs=pl.BlockSpec((tm, tn), lambda n, m: (m, n)),
        out_shape=jax.ShapeDtypeStruct((m_dim, n_out), out_dtype),
        compiler_params=_params("parallel", "parallel"),
        name=name,
    )(*args)


def _matmul_swiglu(a, w_gu, prefix, tm=1024, tn=512, name="mm_swiglu"):
    m_dim, k = a.shape
    f = w_gu.shape[-1] // 2
    tm, tn = min(tm, m_dim), min(tn, f)
    grid = (f // tn, m_dim // tm)
    return pl.pallas_call(
        _mm_swiglu_kernel,
        grid=grid,
        in_specs=[
            pl.BlockSpec((tm, k), lambda n, m: (m, 0)),
            _w_spec(w_gu, k, tn, prefix),
            _w_spec(w_gu, k, tn, prefix, col_block_offset=f // tn),
        ],
        out_specs=pl.BlockSpec((tm, tn), lambda n, m: (m, n)),
        out_shape=jax.ShapeDtypeStruct((m_dim, f), BF16),
        compiler_params=_params("parallel", "parallel"),
        name=name,
    )(a, w_gu, w_gu)


def _mm_down_kernel(a_ref, w_ref, r_ref, o_ref, acc_ref, *, nk):
    k = pl.program_id(1)

    @pl.when(k == 0)
    def _():
        acc_ref[...] = jnp.zeros_like(acc_ref)

    acc_ref[...] += _dot(a_ref[...], w_ref[...].astype(BF16))

    @pl.when(k == nk - 1)
    def _():
        o_ref[...] = r_ref[...] + acc_ref[...]


def _matmul_down(a, w, prefix, residual, tm=1024, tk=512, name="mm_down"):
    m_dim, k_dim = a.shape
    n = w.shape[-1]
    tm, tk = min(tm, m_dim), min(tk, k_dim)
    nk = k_dim // tk
    lead = (None,) * len(prefix)
    return pl.pallas_call(
        functools.partial(_mm_down_kernel, nk=nk),
        grid=(m_dim // tm, nk),
        in_specs=[
            pl.BlockSpec((tm, tk), lambda m, k: (m, k)),
            pl.BlockSpec(lead + (tk, n), lambda m, k: tuple(prefix) + (k, 0)),
            pl.BlockSpec((tm, n), lambda m, k: (m, 0)),
        ],
        out_specs=pl.BlockSpec((tm, n), lambda m, k: (m, 0)),
        out_shape=jax.ShapeDtypeStruct((m_dim, n), F32),
        scratch_shapes=[pltpu.VMEM((tm, n), F32)],
        compiler_params=_params("parallel", "arbitrary"),
        name=name,
    )(a, w, residual)


def _strict_lower_ones(t):
    j = lax.broadcasted_iota(jnp.int32, (t, t), 0)
    s = lax.broadcasted_iota(jnp.int32, (t, t), 1)
    return jnp.where(j > s, 1.0, 0.0).astype(BF16)


def _sb_attn_kernel(q_ref, k_ref, v_ref, o_ref, acc_ref, run_ref, *, tq, tk, hd, g, scale):
    i = pl.program_id(2)
    r = tq // tk
    tri = _strict_lower_ones(tk)
    row = lax.broadcasted_iota(jnp.int32, (tq, tk), 0)
    col = lax.broadcasted_iota(jnp.int32, (tq, tk), 1)

    def block(j, mask):
        start = pl.multiple_of(j * tk, tk)
        for hh in range(g):
            cols = slice(hh * hd, (hh + 1) * hd)
            kb = k_ref[pl.ds(start, tk), cols]
            vb = v_ref[pl.ds(start, tk), cols]
            z = _dot_nt(q_ref[:, cols], kb) * scale
            ls = _log_sigmoid(z)
            l1m = ls - z
            if mask is not None:
                l1m = jnp.where(mask, l1m, 0.0)
            hi = l1m.astype(BF16)
            lo = (l1m - hi.astype(F32)).astype(BF16)
            suffix = _dot(hi, tri) + _dot(lo, tri)
            w = jnp.exp(ls + suffix + run_ref[hh])
            if mask is not None:
                w = jnp.where(mask, w, 0.0)
            acc_ref[:, cols] += _dot(w.astype(BF16), vb)
            run_ref[hh] += jnp.sum(l1m, axis=-1, keepdims=True)

    acc_ref[...] = jnp.zeros_like(acc_ref)
    run_ref[...] = jnp.zeros_like(run_ref)
    for d in reversed(range(r)):
        block(i * r + d, col + d * tk < row)

    def body(t, carry):
        block(i * r - 1 - t, None)
        return carry

    lax.fori_loop(0, i * r, body, 0)
    o_ref[...] = acc_ref[...].astype(o_ref.dtype)


def _sb_attention(qkv, batch, seq, n_heads, head_dim, tq=ATTN_TQ, tk=ATTN_TK,
                  g=ATTN_HEADS_PER_STEP):
    tq, tk = min(tq, seq), min(tk, seq)
    nq = seq // tq
    g = min(g, n_heads)
    ng = n_heads // g
    w = g * head_dim
    return pl.pallas_call(
        functools.partial(_sb_attn_kernel, tq=tq, tk=tk, hd=head_dim, g=g,
                          scale=head_dim ** -0.5),
        grid=(batch, ng, nq),
        in_specs=[
            pl.BlockSpec((tq, w), lambda b, h, i: (b * nq + i, h)),
            pl.BlockSpec((seq, w), lambda b, h, i: (b, ng + h)),
            pl.BlockSpec((seq, w), lambda b, h, i: (b, 2 * ng + h)),
        ],
        out_specs=pl.BlockSpec((tq, w), lambda b, h, i: (b * nq + i, h)),
        out_shape=jax.ShapeDtypeStruct((batch * seq, n_heads * head_dim), BF16),
        scratch_shapes=[pltpu.VMEM((tq, w), F32), pltpu.VMEM((g, tq, 1), F32)],
        compiler_params=_params("parallel", "parallel", "parallel"),
        name="sb_attention",
    )(qkv, qkv, qkv)


def _fgate_cumsum_kernel(fl_ref, bf_ref, c_ref, *, seq):
    x = _log_sigmoid(fl_ref[...] + bf_ref[...])
    j = lax.broadcasted_iota(jnp.int32, (seq, seq), 0)
    t = lax.broadcasted_iota(jnp.int32, (seq, seq), 1)
    upper = jnp.where(j <= t, 1.0, 0.0).astype(BF16)
    hi, mid, lo = _split3(x)
    c_ref[...] = _dot(hi, upper) + (_dot(mid, upper) + _dot(lo, upper))


def _fgate_cumsum(flogit_t, b_f, batch, seq):
    n_heads = flogit_t.shape[0]
    return pl.pallas_call(
        functools.partial(_fgate_cumsum_kernel, seq=seq),
        grid=(batch,),
        in_specs=[
            pl.BlockSpec((n_heads, seq), lambda b: (0, b)),
            pl.BlockSpec((n_heads, 1), lambda b: (0, 0)),
        ],
        out_specs=pl.BlockSpec((None, n_heads, seq), lambda b: (b, 0, 0)),
        out_shape=jax.ShapeDtypeStruct((batch, n_heads, seq), F32),
        compiler_params=_params("parallel"),
        name="fgate_cumsum",
    )(flogit_t, b_f.reshape(n_heads, 1).astype(F32))


def _fox_attn_kernel(q_ref, k_ref, v_ref, cq_ref, ck_ref, o_ref, acc_ref, bias_ref, red_ref,
                     *, tq, tk, hd, g, scale):
    i = pl.program_id(2)
    r = tq // tk
    row = lax.broadcasted_iota(jnp.int32, (tq, tk), 0)
    col = lax.broadcasted_iota(jnp.int32, (tq, tk), 1)
    band_masks = [col + d * tk <= row for d in range(r)]

    for hh in range(g):
        cols = slice(hh * hd, (hh + 1) * hd)

        def logits(j, mask, cols=cols, hh=hh):
            start = pl.multiple_of(j * tk, tk)
            s = (_dot_nt(q_ref[:, cols], k_ref[pl.ds(start, tk), cols]) * scale
                 + bias_ref[...] - ck_ref[hh, :, pl.ds(start, tk)])
            if mask is not None:
                s = jnp.where(mask, s, -jnp.inf)
            return s, start

        bias_ref[...] = jnp.broadcast_to(cq_ref[hh], (tq, tk))
        red_ref[...] = jnp.full((tq, tk), -jnp.inf, F32)

        def max_step(j, mask):
            s, _ = logits(j, mask)
            red_ref[...] = jnp.maximum(red_ref[...], s)

        for d in range(r):
            max_step(i * r + d, band_masks[d])
        lax.fori_loop(0, i * r, lambda t, c: (max_step(t, None), c)[1], 0)
        row_max = jnp.max(red_ref[...], axis=-1, keepdims=True)

        bias_ref[...] = jnp.broadcast_to(cq_ref[hh] - row_max, (tq, tk))
        red_ref[...] = jnp.zeros((tq, tk), F32)
        acc_ref[...] = jnp.zeros((tq, hd), F32)

        def pv_step(j, mask, cols=cols):
            s, start = logits(j, mask)
            p = jnp.exp(s)
            red_ref[...] += p
            acc_ref[...] += _dot(p.astype(BF16), v_ref[pl.ds(start, tk), cols])

        for d in range(r):
            pv_step(i * r + d, band_masks[d])
        lax.fori_loop(0, i * r, lambda t, c: (pv_step(t, None), c)[1], 0)
        denom = jnp.sum(red_ref[...], axis=-1, keepdims=True)
        o_ref[:, cols] = (acc_ref[...] / denom).astype(o_ref.dtype)


def _fox_attention(q, kv, cum_log_f, batch, seq, n_heads, head_dim, tq=ATTN_TQ, tk=ATTN_TK,
                   g=ATTN_HEADS_PER_STEP):
    tq, tk = min(tq, seq), min(tk, seq)
    nq = seq // tq
    g = min(g, n_heads)
    ng = n_heads // g
    w = g * head_dim
    cq = cum_log_f.reshape(batch, n_heads, seq, 1)
    ck = cum_log_f.reshape(batch, n_heads, 1, seq)
    return pl.pallas_call(
        functools.partial(_fox_attn_kernel, tq=tq, tk=tk, hd=head_dim, g=g,
                          scale=head_dim ** -0.5),
        grid=(batch, ng, nq),
        in_specs=[
            pl.BlockSpec((tq, w), lambda b, h, i: (b * nq + i, h)),
            pl.BlockSpec((seq, w), lambda b, h, i: (b, h)),
            pl.BlockSpec((seq, w), lambda b, h, i: (b, ng + h)),
            pl.BlockSpec((None, g, tq, 1), lambda b, h, i: (b, h, i, 0)),
            pl.BlockSpec((None, g, 1, seq), lambda b, h, i: (b, h, 0, 0)),
        ],
        out_specs=pl.BlockSpec((tq, w), lambda b, h, i: (b * nq + i, h)),
        out_shape=jax.ShapeDtypeStruct((batch * seq, n_heads * head_dim), BF16),
        scratch_shapes=[
            pltpu.VMEM((tq, head_dim), F32),
            pltpu.VMEM((tq, tk), F32),
            pltpu.VMEM((tq, tk), F32),
        ],
        compiler_params=_params("parallel", "parallel", "parallel"),
        name="fox_attention",
    )(q, kv, kv, cq, ck)


def _route_kernel(l_ref, w_ref, pos_ref, meta_ref, *, tm, chunk):
    l = l_ref[...]
    n_exp, t = l.shape
    e_idx = lax.broadcasted_iota(jnp.int32, l.shape, 0)
    m1 = jnp.max(l, axis=0, keepdims=True)
    i1 = jnp.min(jnp.where(l == m1, e_idx, n_exp), axis=0, keepdims=True)
    sel1 = e_idx == i1
    l2 = jnp.where(sel1, -jnp.inf, l)
    m2 = jnp.max(l2, axis=0, keepdims=True)
    i2 = jnp.min(jnp.where(l2 == m2, e_idx, n_exp), axis=0, keepdims=True)
    sel2 = e_idx == i2
    e2 = jnp.exp(m2 - m1)
    denom = 1.0 + e2
    w_ref[0:1, :] = 1.0 / denom
    w_ref[1:2, :] = e2 / denom

    s1 = jnp.where(sel1, 1.0, 0.0)
    s2 = jnp.where(sel2, 1.0, 0.0)
    s_any = (s1 + s2).astype(BF16)
    cnt_col = jnp.sum(s1 + s2, axis=1, keepdims=True)
    cnt_row = _dot_nt(jnp.ones((n_exp, t), BF16), s_any)

    def padded(c):
        return jnp.floor((c + (tm - 1)) * (1.0 / tm)) * tm

    ei = lax.broadcasted_iota(jnp.int32, (n_exp, n_exp), 0)
    ej = lax.broadcasted_iota(jnp.int32, (n_exp, n_exp), 1)
    start_col = jnp.sum(jnp.where(ej < ei, padded(cnt_row), 0.0), axis=1, keepdims=True)
    end_col = start_col + padded(cnt_col)

    jj = lax.broadcasted_iota(jnp.int32, (chunk, chunk), 0)
    tt = lax.broadcasted_iota(jnp.int32, (chunk, chunk), 1)
    before = jnp.where(jj < tt, 1.0, 0.0).astype(BF16)
    carry = jnp.zeros((n_exp, 1), F32)
    for c in range(t // chunk):
        sl = slice(c * chunk, (c + 1) * chunk)
        dest = start_col + carry + _dot(s_any[:, sl], before)
        pos_ref[0:1, sl] = jnp.sum(s1[:, sl] * dest, axis=0, keepdims=True).astype(jnp.int32)
        pos_ref[1:2, sl] = jnp.sum(s2[:, sl] * dest, axis=0, keepdims=True).astype(jnp.int32)
        carry = carry + jnp.sum(s1[:, sl] + s2[:, sl], axis=1, keepdims=True)

    lanes = meta_ref.shape[1]
    blk_start = lax.broadcasted_iota(jnp.int32, (n_exp, lanes), 1).astype(F32) * tm
    blk_expert = jnp.sum(jnp.where(end_col <= blk_start, 1.0, 0.0), axis=0, keepdims=True)
    blk_expert = jnp.minimum(blk_expert, n_exp - 1.0)
    n_used = jnp.max(end_col, axis=0, keepdims=True) * (1.0 / tm)
    meta_row = lax.broadcasted_iota(jnp.int32, meta_ref.shape, 0)
    meta_ref[...] = jnp.where(meta_row == 0, blk_expert, n_used).astype(jnp.int32)


def _route(logits_t, tm):
    n_exp, t = logits_t.shape
    assert tm & (tm - 1) == 0, "row tile must be a power of two"
    assert TOP_K * t // tm + n_exp <= VREG_LANES
    chunk = min(1024, t)
    return pl.pallas_call(
        functools.partial(_route_kernel, tm=tm, chunk=chunk),
        out_shape=[
            jax.ShapeDtypeStruct((TOP_K, t), F32),
            jax.ShapeDtypeStruct((TOP_K, t), jnp.int32),
            jax.ShapeDtypeStruct((VREG_SUBLANES, VREG_LANES), jnp.int32),
        ],
        compiler_params=pltpu.CompilerParams(vmem_limit_bytes=V7X_VMEM_LIMIT_BYTES),
        name="moe_route",
    )(logits_t)


def _dispatch_kernel(pos_ref, x_hbm, init_hbm, o_hbm, sem, *, tb):
    del init_hbm
    base = pl.program_id(0) * tb

    def issue(r, carry):
        for c in range(TOP_K):
            pltpu.make_async_copy(x_hbm.at[base + r], o_hbm.at[pos_ref[c, r]], sem).start()
        return carry

    lax.fori_loop(0, tb, issue, 0)
    pltpu.make_async_copy(
        x_hbm.at[pl.ds(0, TOP_K * tb)], o_hbm.at[pl.ds(0, TOP_K * tb)], sem).wait()


def _dispatch(pos, x, m_pad, tb=256):
    t, d = x.shape
    tb = min(tb, t)
    x3 = x.reshape(t, d // VREG_LANES, VREG_LANES)
    out = pl.pallas_call(
        functools.partial(_dispatch_kernel, tb=tb),
        grid=(t // tb,),
        in_specs=[
            pl.BlockSpec((TOP_K, tb), lambda i: (0, i), memory_space=pltpu.SMEM),
            pl.BlockSpec(memory_space=pl.ANY),
            pl.BlockSpec(memory_space=pl.ANY),
        ],
        out_specs=pl.BlockSpec(memory_space=pl.ANY),
        out_shape=jax.ShapeDtypeStruct((m_pad,) + x3.shape[1:], x.dtype),
        scratch_shapes=[pltpu.SemaphoreType.DMA(())],
        input_output_aliases={2: 0},
        compiler_params=_params("arbitrary"),
        name="moe_dispatch",
    )(pos, x3, jnp.zeros((m_pad,) + x3.shape[1:], x.dtype))
    return out.reshape(m_pad, d)


def _clamp_block(b, n_used_ref):
    return jnp.minimum(b, n_used_ref[0] - 1)


def _gmm_swiglu_kernel(be_ref, nu_ref, a_ref, wg_ref, wu_ref, o_ref):
    del be_ref

    used = pl.program_id(1) < nu_ref[0]

    @pl.when(used)
    def _():
        _mm_swiglu_kernel(a_ref, wg_ref, wu_ref, o_ref)

    @pl.when(jnp.logical_not(used))
    def _():
        o_ref[...] = jnp.zeros_like(o_ref)


def _grouped_swiglu(blk_expert, n_used, a, w_gu, layer, tm, tn=512):
    m_pad, k = a.shape
    f = w_gu.shape[-1] // 2
    tn = min(tn, f)
    nb = m_pad // tm

    def a_map(n, b, be, nu):
        return (_clamp_block(b, nu), 0)

    def w_map(off):
        return lambda n, b, be, nu: (layer, be[_clamp_block(b, nu)], 0, n + off)

    return pl.pallas_call(
        _gmm_swiglu_kernel,
        grid_spec=pltpu.PrefetchScalarGridSpec(
            num_scalar_prefetch=2,
            grid=(f // tn, nb),
            in_specs=[
                pl.BlockSpec((tm, k), a_map),
                pl.BlockSpec((None, None, k, tn), w_map(0)),
                pl.BlockSpec((None, None, k, tn), w_map(f // tn)),
            ],
            out_specs=pl.BlockSpec((tm, tn), lambda n, b, be, nu: (b, n)),
        ),
        out_shape=jax.ShapeDtypeStruct((m_pad, f), BF16),
        compiler_params=_params("parallel", "arbitrary"),
        name="moe_gu",
    )(blk_expert, n_used, a, w_gu, w_gu)


def _gmm_down_kernel(be_ref, nu_ref, a_ref, w_ref, o_ref, acc_ref, *, nk):
    del be_ref
    k = pl.program_id(1)
    used = pl.program_id(0) < nu_ref[0]

    @pl.when(used)
    def _():
        @pl.when(k == 0)
        def _():
            acc_ref[...] = jnp.zeros_like(acc_ref)

        acc_ref[...] += _dot(a_ref[...], w_ref[...].astype(BF16))

        @pl.when(k == nk - 1)
        def _():
            o_ref[...] = acc_ref[...]

    @pl.when(jnp.logical_not(used) & (k == 0))
    def _():
        o_ref[...] = jnp.zeros_like(o_ref)


def _grouped_down(blk_expert, n_used, a, w_down, layer, tm, tk=1024):
    m_pad, k_dim = a.shape
    n = w_down.shape[-1]
    tk = min(tk, k_dim)
    nk = k_dim // tk
    nb = m_pad // tm

    def k_eff(b, k, nu):
        return jnp.where(b < nu[0], k, nk - 1)

    return pl.pallas_call(
        functools.partial(_gmm_down_kernel, nk=nk),
        grid_spec=pltpu.PrefetchScalarGridSpec(
            num_scalar_prefetch=2,
            grid=(nb, nk),
            in_specs=[
                pl.BlockSpec((tm, tk), lambda b, k, be, nu: (_clamp_block(b, nu), k_eff(b, k, nu))),
                pl.BlockSpec((None, None, tk, n),
                             lambda b, k, be, nu: (layer, be[_clamp_block(b, nu)],
                                                   k_eff(b, k, nu), 0)),
            ],
            out_specs=pl.BlockSpec((tm, n), lambda b, k, be, nu: (b, 0)),
            scratch_shapes=[pltpu.VMEM((tm, n), F32)],
        ),
        out_shape=jax.ShapeDtypeStruct((m_pad, n), F32),
        compiler_params=_params("arbitrary", "arbitrary"),
        name="moe_down",
    )(blk_expert, n_used, a, w_down)


def _combine_kernel(pos_ref, y_hbm, h_ref, w_ref, g_ref, o_ref, buf_ref, sem, *, tb):
    def issue(r, carry):
        for c in range(TOP_K):
            pltpu.make_async_copy(
                y_hbm.at[pl.ds(pos_ref[c, r], 1)], buf_ref.at[c, pl.ds(r, 1)], sem).start()
        return carry

    lax.fori_loop(0, tb, issue, 0)
    for c in range(TOP_K):
        pltpu.make_async_copy(y_hbm.at[pl.ds(0, tb)], buf_ref.at[c], sem).wait()
    w = w_ref[...]
    h = h_ref[...] + (w[:, 0:1] * buf_ref[0] + w[:, 1:2] * buf_ref[1])
    y = h * lax.rsqrt(jnp.mean(h * h, axis=-1, keepdims=True) + RMS_EPS)
    o_ref[...] = (y * g_ref[...]).astype(o_ref.dtype)


def _combine_norm(pos, y_sorted, h, gate_w, gain, out_dtype, tb=256):
    t, d = h.shape
    tb = min(tb, t)
    return pl.pallas_call(
        functools.partial(_combine_kernel, tb=tb),
        grid=(t // tb,),
        in_specs=[
            pl.BlockSpec((TOP_K, tb), lambda i: (0, i), memory_space=pltpu.SMEM),
            pl.BlockSpec(memory_space=pl.ANY),
            pl.BlockSpec((tb, d), lambda i: (i, 0)),
            pl.BlockSpec((tb, TOP_K), lambda i: (i, 0)),
            pl.BlockSpec((1, d), lambda i: (0, 0)),
        ],
        out_specs=pl.BlockSpec((tb, d), lambda i: (i, 0)),
        out_shape=jax.ShapeDtypeStruct((t, d), out_dtype),
        scratch_shapes=[pltpu.VMEM((TOP_K, tb, d), F32), pltpu.SemaphoreType.DMA(())],
        compiler_params=_params("arbitrary"),
        name="moe_combine_norm",
    )(pos, y_sorted, h, gate_w, gain.reshape(1, d).astype(F32))


def kernel(x, norm_mix, norm_ffn, w_qkv_a, w_o_a, norm_kv, w_kvf_b, b_f, w_q_b, w_o_b,
           w_gu_dense, w_down_dense, w_router, w_gu_exp, w_down_exp, norm_final):
    batch, seq, d_model = x.shape
    n_heads = b_f.shape[0]
    d_attn = w_o_a.shape[1]
    head_dim = d_attn // n_heads
    n_experts = w_router.shape[-1]
    assert TOP_K == 2 and norm_mix.shape[0] == 2, "kernel is written for the depth-2 trunk"
    t = batch * seq
    h = x.reshape(t, d_model)

    (hn,) = _rmsnorm(h, [norm_mix[0]], BF16)
    qkv = _matmul(hn, w_qkv_a, (0,), 3 * d_attn, BF16, name="mm_qkv_a")
    o = _sb_attention(qkv, batch, seq, n_heads, head_dim)
    h = _matmul(o, w_o_a, (0,), d_model, F32, residual=h, name="mm_o_a")
    (hn,) = _rmsnorm(h, [norm_ffn[0]], BF16)
    act = _matmul_swiglu(hn, w_gu_dense, (0,), name="mm_gu_dense")
    h = _matmul_down(act, w_down_dense, (0,), h, name="mm_down_dense")

    w_f_t = w_kvf_b[:, 2 * d_attn:].T
    hn_q, hn_kv, flogit_t = _rmsnorm(h, [norm_mix[1], norm_kv], BF16, proj_t=w_f_t)
    kv = _matmul(hn_kv, w_kvf_b, (), 2 * d_attn, BF16, name="mm_kv_b")
    q = _matmul(hn_q, w_q_b, (0,), d_attn, BF16, name="mm_q_b")
    cum_log_f = _fgate_cumsum(flogit_t, b_f, batch, seq)
    o = _fox_attention(q, kv, cum_log_f, batch, seq, n_heads, head_dim)
    h = _matmul(o, w_o_b, (0,), d_model, F32, residual=h, name="mm_o_b")

    hn, logits_t = _rmsnorm(h, [norm_ffn[1]], BF16, proj_t=w_router[0].T)
    tm = min(MOE_ROW_TILE, t)
    m_pad = TOP_K * t + n_experts * tm
    gate_w, pos, meta = _route(logits_t, tm)
    blk_expert, n_used = meta[0, :m_pad // tm], meta[1, :1]
    x_sorted = _dispatch(pos, hn, m_pad)
    act = _grouped_swiglu(blk_expert, n_used, x_sorted, w_gu_exp, 0, tm)
    y_sorted = _grouped_down(blk_expert, n_used, act, w_down_exp, 0, tm)
    out = _combine_norm(pos, y_sorted, h, gate_w.T, norm_final, x.dtype)
    return out.reshape(batch, seq, d_model)
```

```python
import functools

import jax
import jax.numpy as jnp
from jax import lax
from jax.experimental import pallas as pl
from jax.experimental.pallas import tpu as pltpu

RMS_EPS = 1e-6
TOP_K = 2

V7X_VMEM_LIMIT_BYTES = 56 * 1024 * 1024

VREG_SUBLANES = 8
VREG_LANES = 128

MOE_ROW_TILE = 512
ATTN_TQ = 512
ATTN_TK = 256

BF16 = jnp.bfloat16
F32 = jnp.float32


def _params(*semantics):
    return pltpu.CompilerParams(
        dimension_semantics=semantics, vmem_limit_bytes=V7X_VMEM_LIMIT_BYTES)


def _dot(a, b):
    return jnp.dot(a, b, preferred_element_type=F32)


def _dot_nt(a, b):
    return lax.dot_general(a, b, (((1,), (1,)), ((), ())), preferred_element_type=F32)


def _split3(x):
    hi = x.astype(BF16)
    r1 = x - hi.astype(F32)
    mid = r1.astype(BF16)
    lo = (r1 - mid.astype(F32)).astype(BF16)
    return hi, mid, lo


def _log_sigmoid(z):
    return jnp.minimum(z, 0.0) - jnp.log(1.0 + jnp.exp(-jnp.abs(z)))


def _rmsnorm_kernel(*refs, n_gains, has_proj):
    x_ref = refs[0]
    g_refs = refs[1:1 + n_gains]
    pos = 1 + n_gains
    p_ref = refs[pos] if has_proj else None
    pos += int(has_proj)
    o_refs = refs[pos:pos + n_gains]
    po_ref = refs[pos + n_gains] if has_proj else None

    x = x_ref[...]
    y = x * lax.rsqrt(jnp.mean(x * x, axis=-1, keepdims=True) + RMS_EPS)
    for g_ref, o_ref in zip(g_refs, o_refs):
        o_ref[...] = (y * g_ref[...]).astype(o_ref.dtype)
    if has_proj:
        yn = y * g_refs[-1][...]
        yh, ym, _ = _split3(yn)
        ph, pm, _ = _split3(p_ref[...])
        po_ref[...] = _dot_nt(ph, yh) + (_dot_nt(ph, ym) + _dot_nt(pm, yh))


def _rmsnorm(x, gains, out_dtype, proj_t=None, tm=256):
    t, d = x.shape
    tm = min(tm, t)
    n_gains = len(gains)
    has_proj = proj_t is not None
    in_specs = [pl.BlockSpec((tm, d), lambda i: (i, 0))]
    in_specs += [pl.BlockSpec((1, d), lambda i: (0, 0))] * n_gains
    args = [x] + [g.reshape(1, d).astype(F32) for g in gains]
    out_shape = [jax.ShapeDtypeStruct((t, d), out_dtype)] * n_gains
    out_specs = [pl.BlockSpec((tm, d), lambda i: (i, 0))] * n_gains
    if has_proj:
        n = proj_t.shape[0]
        in_specs.append(pl.BlockSpec((n, d), lambda i: (0, 0)))
        args.append(proj_t)
        out_shape.append(jax.ShapeDtypeStruct((n, t), F32))
        out_specs.append(pl.BlockSpec((n, tm), lambda i: (0, i)))
    outs = pl.pallas_call(
        functools.partial(_rmsnorm_kernel, n_gains=n_gains, has_proj=has_proj),
        grid=(t // tm,),
        in_specs=in_specs,
        out_specs=out_specs,
        out_shape=out_shape,
        compiler_params=_params("parallel"),
        name="rmsnorm",
    )(*args)
    return outs


def _mm_plain_kernel(a_ref, w_ref, o_ref):
    o_ref[...] = _dot(a_ref[...], w_ref[...].astype(BF16)).astype(o_ref.dtype)


def _mm_res_kernel(a_ref, w_ref, r_ref, o_ref):
    o_ref[...] = r_ref[...] + _dot(a_ref[...], w_ref[...].astype(BF16))


def _mm_swiglu_kernel(a_ref, wg_ref, wu_ref, o_ref):
    a = a_ref[...]
    g = _dot(a, wg_ref[...].astype(BF16))
    u = _dot(a, wu_ref[...].astype(BF16))
    o_ref[...] = (g * jax.nn.sigmoid(g) * u).astype(o_ref.dtype)


def _w_spec(w, k, tn, prefix, col_block_offset=0):
    lead = (None,) * len(prefix)
    return pl.BlockSpec(
        lead + (k, tn), lambda n, m: tuple(prefix) + (0, n + col_block_offset))


def _matmul(a, w, prefix, n_out, out_dtype, residual=None, tm=1024, tn=1024, name="mm"):
    m_dim, k = a.shape
    tm, tn = min(tm, m_dim), min(tn, n_out)
    grid = (n_out // tn, m_dim // tm)
    in_specs = [pl.BlockSpec((tm, k), lambda n, m: (m, 0)), _w_spec(w, k, tn, prefix)]
    args = [a, w]
    kernel = _mm_plain_kernel
    if residual is not None:
        in_specs.append(pl.BlockSpec((tm, tn), lambda n, m: (m, n)))
        args.append(residual)
        kernel = _mm_res_kernel
    return pl.pallas_call(
        kernel,
        grid=grid,
        in_specs=in_specs,
        out_specs=pl.BlockSpec((tm, tn), lambda n, m: (m, n)),
        out_shape=jax.ShapeDtypeStruct((m_dim, n_out), out_dtype),
        compiler_params=_params("parallel", "parallel"),
        name=name,
    )(*args)


def _matmul_swiglu(a, w_gu, prefix, tm=1024, tn=512, name="mm_swiglu"):
    m_dim, k = a.shape
    f = w_gu.shape[-1] // 2
    tm, tn = min(tm, m_dim), min(tn, f)
    grid = (f // tn, m_dim // tm)
    return pl.pallas_call(
        _mm_swiglu_kernel,
        grid=grid,
        in_specs=[
            pl.BlockSpec((tm, k), lambda n, m: (m, 0)),
            _w_spec(w_gu, k, tn, prefix),
            _w_spec(w_gu, k, tn, prefix, col_block_offset=f // tn),
        ],
        out_specs=pl.BlockSpec((tm, tn), lambda n, m: (m, n)),
        out_shape=jax.ShapeDtypeStruct((m_dim, f), BF16),
        compiler_params=_params("parallel", "parallel"),
        name=name,
    )(a, w_gu, w_gu)


def _mm_down_kernel(a_ref, w_ref, r_ref, o_ref, acc_ref, *, nk):
    k = pl.program_id(1)

    @pl.when(k == 0)
    def _():
        acc_ref[...] = jnp.zeros_like(acc_ref)

    acc_ref[...] += _dot(a_ref[...], w_ref[...].astype(BF16))

    @pl.when(k == nk - 1)
    def _():
        o_ref[...] = r_ref[...] + acc_ref[...]


def _matmul_down(a, w, prefix, residual, tm=1024, tk=512, name="mm_down"):
    m_dim, k_dim = a.shape
    n = w.shape[-1]
    tm, tk = min(tm, m_dim), min(tk, k_dim)
    nk = k_dim // tk
    lead = (None,) * len(prefix)
    return pl.pallas_call(
        functools.partial(_mm_down_kernel, nk=nk),
        grid=(m_dim // tm, nk),
        in_specs=[
            pl.BlockSpec((tm, tk), lambda m, k: (m, k)),
            pl.BlockSpec(lead + (tk, n), lambda m, k: tuple(prefix) + (k, 0)),
            pl.BlockSpec((tm, n), lambda m, k: (m, 0)),
        ],
        out_specs=pl.BlockSpec((tm, n), lambda m, k: (m, 0)),
        out_shape=jax.ShapeDtypeStruct((m_dim, n), F32),
        scratch_shapes=[pltpu.VMEM((tm, n), F32)],
        compiler_params=_params("parallel", "arbitrary"),
        name=name,
    )(a, w, residual)


def _strict_lower_ones(t):
    j = lax.broadcasted_iota(jnp.int32, (t, t), 0)
    s = lax.broadcasted_iota(jnp.int32, (t, t), 1)
    return jnp.where(j > s, 1.0, 0.0).astype(BF16)


def _sb_attn_kernel(q_ref, k_ref, v_ref, o_ref, acc_ref, run_ref, ls_ref, hi_ref, lo_ref,
                    rs_ref, *, tq, tk, scale):
    i = pl.program_id(2)
    r = tq // tk
    tri = _strict_lower_ones(tk)
    row = lax.broadcasted_iota(jnp.int32, (tq, tk), 0)
    col = lax.broadcasted_iota(jnp.int32, (tq, tk), 1)

    def score_stage(j, mask, slot):
        start = pl.multiple_of(j * tk, tk)
        z = _dot_nt(q_ref[...], k_ref[pl.ds(start, tk), :]) * scale
        ls = _log_sigmoid(z)
        l1m = ls - z
        if mask is not None:
            l1m = jnp.where(mask, l1m, 0.0)
            ls = jnp.where(mask, ls, -jnp.inf)
        hi = l1m.astype(BF16)
        ls_ref[slot] = ls
        hi_ref[slot] = hi
        lo_ref[slot] = (l1m - hi.astype(F32)).astype(BF16)
        rs_ref[slot] = jnp.sum(l1m, axis=-1, keepdims=True)

    def value_stage(j, slot):
        start = pl.multiple_of(j * tk, tk)
        suffix = _dot(hi_ref[slot], tri) + _dot(lo_ref[slot], tri)
        w = jnp.exp(ls_ref[slot] + suffix + run_ref[...])
        acc_ref[...] += _dot(w.astype(BF16), v_ref[pl.ds(start, tk), :])
        run_ref[...] += rs_ref[slot]

    acc_ref[...] = jnp.zeros_like(acc_ref)
    run_ref[...] = jnp.zeros_like(run_ref)
    prev = None
    for d in reversed(range(r)):
        score_stage(i * r + d, col + d * tk < row, d)
        if prev is not None:
            value_stage(*prev)
        prev = (i * r + d, d)

    def body(t, carry):
        base = (i - 1 - t) * r
        pending = (base + r, 0)
        for d in reversed(range(r)):
            score_stage(base + d, None, d)
            value_stage(*pending)
            pending = (base + d, d)
        return carry

    lax.fori_loop(0, i, body, 0)
    value_stage(0, 0)
    o_ref[...] = acc_ref[...].astype(o_ref.dtype)


def _sb_attention(qkv, batch, seq, n_heads, head_dim, tq=ATTN_TQ, tk=ATTN_TK):
    tq, tk = min(tq, seq), min(tk, seq)
    nq = seq // tq
    r = tq // tk
    return pl.pallas_call(
        functools.partial(_sb_attn_kernel, tq=tq, tk=tk, scale=head_dim ** -0.5),
        grid=(batch, n_heads, nq),
        in_specs=[
            pl.BlockSpec((tq, head_dim), lambda b, h, i: (b * nq + i, h)),
            pl.BlockSpec((seq, head_dim), lambda b, h, i: (b, n_heads + h)),
            pl.BlockSpec((seq, head_dim), lambda b, h, i: (b, 2 * n_heads + h)),
        ],
        out_specs=pl.BlockSpec((tq, head_dim), lambda b, h, i: (b * nq + i, h)),
        out_shape=jax.ShapeDtypeStruct((batch * seq, n_heads * head_dim), BF16),
        scratch_shapes=[
            pltpu.VMEM((tq, head_dim), F32),
            pltpu.VMEM((tq, 1), F32),
            pltpu.VMEM((r, tq, tk), F32),
            pltpu.VMEM((r, tq, tk), BF16),
            pltpu.VMEM((r, tq, tk), BF16),
            pltpu.VMEM((r, tq, 1), F32),
        ],
        compiler_params=_params("parallel", "parallel", "parallel"),
        name="sb_attention",
    )(qkv, qkv, qkv)


def _fgate_cumsum_kernel(fl_ref, bf_ref, c_ref, *, seq):
    x = _log_sigmoid(fl_ref[...] + bf_ref[...])
    j = lax.broadcasted_iota(jnp.int32, (seq, seq), 0)
    t = lax.broadcasted_iota(jnp.int32, (seq, seq), 1)
    upper = jnp.where(j <= t, 1.0, 0.0).astype(BF16)
    hi, mid, lo = _split3(x)
    c_ref[...] = _dot(hi, upper) + (_dot(mid, upper) + _dot(lo, upper))


def _fgate_cumsum(flogit_t, b_f, batch, seq):
    n_heads = flogit_t.shape[0]
    return pl.pallas_call(
        functools.partial(_fgate_cumsum_kernel, seq=seq),
        grid=(batch,),
        in_specs=[
            pl.BlockSpec((n_heads, seq), lambda b: (0, b)),
            pl.BlockSpec((n_heads, 1), lambda b: (0, 0)),
        ],
        out_specs=pl.BlockSpec((None, n_heads, seq), lambda b: (b, 0, 0)),
        out_shape=jax.ShapeDtypeStruct((batch, n_heads, seq), F32),
        compiler_params=_params("parallel"),
        name="fgate_cumsum",
    )(flogit_t, b_f.reshape(n_heads, 1).astype(F32))


def _fox_attn_kernel(q_ref, k_ref, v_ref, cq_ref, ck_ref, o_ref, acc_ref, bias_ref, red_ref,
                     p_ref, *, tq, tk, scale):
    i = pl.program_id(2)
    r = tq // tk
    row = lax.broadcasted_iota(jnp.int32, (tq, tk), 0)
    col = lax.broadcasted_iota(jnp.int32, (tq, tk), 1)
    band_masks = [col + d * tk <= row for d in range(r)]

    def logits(j, mask):
        start = pl.multiple_of(j * tk, tk)
        s = (_dot_nt(q_ref[...], k_ref[pl.ds(start, tk), :]) * scale
             + bias_ref[...] - ck_ref[:, pl.ds(start, tk)])
        if mask is not None:
            s = jnp.where(mask, s, -jnp.inf)
        return s

    bias_ref[...] = jnp.broadcast_to(cq_ref[...], (tq, tk))
    red_ref[...] = jnp.full((tq, tk), -jnp.inf, F32)

    def max_step(j, mask):
        red_ref[...] = jnp.maximum(red_ref[...], logits(j, mask))

    for d in range(r):
        max_step(i * r + d, band_masks[d])

    def max_body(t, carry):
        for d in range(r):
            max_step(t * r + d, None)
        return carry

    lax.fori_loop(0, i, max_body, 0)
    row_max = jnp.max(red_ref[...], axis=-1, keepdims=True)

    bias_ref[...] = jnp.broadcast_to(cq_ref[...] - row_max, (tq, tk))
    red_ref[...] = jnp.zeros((tq, tk), F32)
    acc_ref[...] = jnp.zeros_like(acc_ref)

    def prob_stage(j, mask, slot):
        p = jnp.exp(logits(j, mask))
        red_ref[...] += p
        p_ref[slot] = p.astype(BF16)

    def value_stage(j, slot):
        start = pl.multiple_of(j * tk, tk)
        acc_ref[...] += _dot(p_ref[slot], v_ref[pl.ds(start, tk), :])

    prev = None
    for d in range(r):
        prob_stage(i * r + d, band_masks[d], d)
        if prev is not None:
            value_stage(*prev)
        prev = (i * r + d, d)

    def pv_body(t, carry):
        base = t * r
        pending = (jnp.where(t == 0, (i + 1) * r - 1, base - 1), r - 1)
        for d in range(r):
            prob_stage(base + d, None, d)
            value_stage(*pending)
            pending = (base + d, d)
        return carry

    lax.fori_loop(0, i, pv_body, 0)
    value_stage(jnp.where(i == 0, r - 1, i * r - 1), r - 1)
    denom = jnp.sum(red_ref[...], axis=-1, keepdims=True)
    o_ref[...] = (acc_ref[...] / denom).astype(o_ref.dtype)


def _fox_attention(q, kv, cum_log_f, batch, seq, n_heads, head_dim, tq=ATTN_TQ, tk=ATTN_TK):
    tq, tk = min(tq, seq), min(tk, seq)
    nq = seq // tq
    r = tq // tk
    cq = cum_log_f.reshape(batch, n_heads, seq, 1)
    ck = cum_log_f.reshape(batch, n_heads, 1, seq)
    return pl.pallas_call(
        functools.partial(_fox_attn_kernel, tq=tq, tk=tk, scale=head_dim ** -0.5),
        grid=(batch, n_heads, nq),
        in_specs=[
            pl.BlockSpec((tq, head_dim), lambda b, h, i: (b * nq + i, h)),
            pl.BlockSpec((seq, head_dim), lambda b, h, i: (b, h)),
            pl.BlockSpec((seq, head_dim), lambda b, h, i: (b, n_heads + h)),
            pl.BlockSpec((None, None, tq, 1), lambda b, h, i: (b, h, i, 0)),
            pl.BlockSpec((None, None, 1, seq), lambda b, h, i: (b, h, 0, 0)),
        ],
        out_specs=pl.BlockSpec((tq, head_dim), lambda b, h, i: (b * nq + i, h)),
        out_shape=jax.ShapeDtypeStruct((batch * seq, n_heads * head_dim), BF16),
        scratch_shapes=[
            pltpu.VMEM((tq, head_dim), F32),
            pltpu.VMEM((tq, tk), F32),
            pltpu.VMEM((tq, tk), F32),
            pltpu.VMEM((r, tq, tk), BF16),
        ],
        compiler_params=_params("parallel", "parallel", "parallel"),
        name="fox_attention",
    )(q, kv, kv, cq, ck)


def _route_kernel(l_ref, w_ref, pos_ref, meta_ref, *, tm, chunk):
    l = l_ref[...]
    n_exp, t = l.shape
    e_idx = lax.broadcasted_iota(jnp.int32, l.shape, 0)
    m1 = jnp.max(l, axis=0, keepdims=True)
    i1 = jnp.min(jnp.where(l == m1, e_idx, n_exp), axis=0, keepdims=True)
    sel1 = e_idx == i1
    l2 = jnp.where(sel1, -jnp.inf, l)
    m2 = jnp.max(l2, axis=0, keepdims=True)
    i2 = jnp.min(jnp.where(l2 == m2, e_idx, n_exp), axis=0, keepdims=True)
    sel2 = e_idx == i2
    e2 = jnp.exp(m2 - m1)
    denom = 1.0 + e2
    w_ref[0:1, :] = 1.0 / denom
    w_ref[1:2, :] = e2 / denom

    s1 = jnp.where(sel1, 1.0, 0.0)
    s2 = jnp.where(sel2, 1.0, 0.0)
    s_any = (s1 + s2).astype(BF16)
    cnt_col = jnp.sum(s1 + s2, axis=1, keepdims=True)
    cnt_row = _dot_nt(jnp.ones((n_exp, t), BF16), s_any)

    def padded(c):
        return jnp.floor((c + (tm - 1)) * (1.0 / tm)) * tm

    ei = lax.broadcasted_iota(jnp.int32, (n_exp, n_exp), 0)
    ej = lax.broadcasted_iota(jnp.int32, (n_exp, n_exp), 1)
    start_col = jnp.sum(jnp.where(ej < ei, padded(cnt_row), 0.0), axis=1, keepdims=True)
    end_col = start_col + padded(cnt_col)

    jj = lax.broadcasted_iota(jnp.int32, (chunk, chunk), 0)
    tt = lax.broadcasted_iota(jnp.int32, (chunk, chunk), 1)
    before = jnp.where(jj < tt, 1.0, 0.0).astype(BF16)
    carry = jnp.zeros((n_exp, 1), F32)
    for c in range(t // chunk):
        sl = slice(c * chunk, (c + 1) * chunk)
        dest = start_col + carry + _dot(s_any[:, sl], before)
        pos_ref[0:1, sl] = jnp.sum(s1[:, sl] * dest, axis=0, keepdims=True).astype(jnp.int32)
        pos_ref[1:2, sl] = jnp.sum(s2[:, sl] * dest, axis=0, keepdims=True).astype(jnp.int32)
        carry = carry + jnp.sum(s1[:, sl] + s2[:, sl], axis=1, keepdims=True)

    lanes = meta_ref.shape[1]
    blk_start = lax.broadcasted_iota(jnp.int32, (n_exp, lanes), 1).astype(F32) * tm
    blk_expert = jnp.sum(jnp.where(end_col <= blk_start, 1.0, 0.0), axis=0, keepdims=True)
    blk_expert = jnp.minimum(blk_expert, n_exp - 1.0)
    n_used = jnp.max(end_col, axis=0, keepdims=True) * (1.0 / tm)
    meta_row = lax.broadcasted_iota(jnp.int32, meta_ref.shape, 0)
    meta_ref[...] = jnp.where(meta_row == 0, blk_expert, n_used).astype(jnp.int32)


def _route(logits_t, tm):
    n_exp, t = logits_t.shape
    assert tm & (tm - 1) == 0, "row tile must be a power of two"
    assert TOP_K * t // tm + n_exp <= VREG_LANES
    chunk = min(1024, t)
    return pl.pallas_call(
        functools.partial(_route_kernel, tm=tm, chunk=chunk),
        out_shape=[
            jax.ShapeDtypeStruct((TOP_K, t), F32),
            jax.ShapeDtypeStruct((TOP_K, t), jnp.int32),
            jax.ShapeDtypeStruct((VREG_SUBLANES, VREG_LANES), jnp.int32),
        ],
        compiler_params=pltpu.CompilerParams(vmem_limit_bytes=V7X_VMEM_LIMIT_BYTES),
        name="moe_route",
    )(logits_t)


def _dispatch_kernel(pos_ref, x_ref, init_hbm, o_hbm, sem, *, tb):
    del init_hbm

    def issue(r, carry):
        for c in range(TOP_K):
            pltpu.make_async_copy(x_ref.at[r], o_hbm.at[pos_ref[c, r]], sem).start()
        return carry

    lax.fori_loop(0, tb, issue, 0)
    for c in range(TOP_K):
        pltpu.make_async_copy(x_ref, o_hbm.at[pl.ds(0, tb)], sem).wait()


def _dispatch(pos, x, m_pad, tb=256):
    t, d = x.shape
    tb = min(tb, t)
    x3 = x.reshape(t, d // VREG_LANES, VREG_LANES)
    out = pl.pallas_call(
        functools.partial(_dispatch_kernel, tb=tb),
        grid=(t // tb,),
        in_specs=[
            pl.BlockSpec((TOP_K, tb), lambda i: (0, i), memory_space=pltpu.SMEM),
            pl.BlockSpec((tb,) + x3.shape[1:], lambda i: (i, 0, 0)),
            pl.BlockSpec(memory_space=pl.ANY),
        ],
        out_specs=pl.BlockSpec(memory_space=pl.ANY),
        out_shape=jax.ShapeDtypeStruct((m_pad,) + x3.shape[1:], x.dtype),
        scratch_shapes=[pltpu.SemaphoreType.DMA(())],
        input_output_aliases={2: 0},
        compiler_params=_params("arbitrary"),
        name="moe_dispatch",
    )(pos, x3, jnp.zeros((m_pad,) + x3.shape[1:], x.dtype))
    return out.reshape(m_pad, d)


def _clamp_block(b, n_used_ref):
    return jnp.minimum(b, n_used_ref[0] - 1)


def _gmm_swiglu_kernel(be_ref, nu_ref, a_ref, wg_ref, wu_ref, o_ref):
    del be_ref

    used = pl.program_id(1) < nu_ref[0]

    @pl.when(used)
    def _():
        _mm_swiglu_kernel(a_ref, wg_ref, wu_ref, o_ref)

    @pl.when(jnp.logical_not(used))
    def _():
        o_ref[...] = jnp.zeros_like(o_ref)


def _grouped_swiglu(blk_expert, n_used, a, w_gu, layer, tm, tn=1024):
    m_pad, k = a.shape
    f = w_gu.shape[-1] // 2
    tn = min(tn, f)
    nb = m_pad // tm

    def a_map(n, b, be, nu):
        return (_clamp_block(b, nu), 0)

    def w_map(off):
        return lambda n, b, be, nu: (layer, be[_clamp_block(b, nu)], 0, n + off)

    return pl.pallas_call(
        _gmm_swiglu_kernel,
        grid_spec=pltpu.PrefetchScalarGridSpec(
            num_scalar_prefetch=2,
            grid=(f // tn, nb),
            in_specs=[
                pl.BlockSpec((tm, k), a_map),
                pl.BlockSpec((None, None, k, tn), w_map(0)),
                pl.BlockSpec((None, None, k, tn), w_map(f // tn)),
            ],
            out_specs=pl.BlockSpec((tm, tn), lambda n, b, be, nu: (b, n)),
        ),
        out_shape=jax.ShapeDtypeStruct((m_pad, f), BF16),
        compiler_params=_params("parallel", "arbitrary"),
        name="moe_gu",
    )(blk_expert, n_used, a, w_gu, w_gu)


def _gmm_down_kernel(be_ref, nu_ref, a_ref, w_ref, o_ref, acc_ref, *, nk):
    del be_ref
    k = pl.program_id(1)
    used = pl.program_id(0) < nu_ref[0]

    @pl.when(used)
    def _():
        @pl.when(k == 0)
        def _():
            acc_ref[...] = jnp.zeros_like(acc_ref)

        acc_ref[...] += _dot(a_ref[...], w_ref[...].astype(BF16))

        @pl.when(k == nk - 1)
        def _():
            o_ref[...] = acc_ref[...]

    @pl.when(jnp.logical_not(used) & (k == 0))
    def _():
        o_ref[...] = jnp.zeros_like(o_ref)


def _grouped_down(blk_expert, n_used, a, w_down, layer, tm, tk=1024):
    m_pad, k_dim = a.shape
    n = w_down.shape[-1]
    tk = min(tk, k_dim)
    nk = k_dim // tk
    nb = m_pad // tm

    def k_eff(b, k, nu):
        return jnp.where(b < nu[0], k, nk - 1)

    return pl.pallas_call(
        functools.partial(_gmm_down_kernel, nk=nk),
        grid_spec=pltpu.PrefetchScalarGridSpec(
            num_scalar_prefetch=2,
            grid=(nb, nk),
            in_specs=[
                pl.BlockSpec((tm, tk), lambda b, k, be, nu: (_clamp_block(b, nu), k_eff(b, k, nu))),
                pl.BlockSpec((None, None, tk, n),
                             lambda b, k, be, nu: (layer, be[_clamp_block(b, nu)],
                                                   k_eff(b, k, nu), 0)),
            ],
            out_specs=pl.BlockSpec((tm, n), lambda b, k, be, nu: (b, 0)),
            scratch_shapes=[pltpu.VMEM((tm, n), F32)],
        ),
        out_shape=jax.ShapeDtypeStruct((m_pad, n), F32),
        compiler_params=_params("arbitrary", "arbitrary"),
        name="moe_down",
    )(blk_expert, n_used, a, w_down)


def _combine_kernel(pos_ref, y_hbm, h_ref, w_ref, g_ref, o_ref, buf_ref, sem, *, tb):
    def issue(r, carry):
        for c in range(TOP_K):
            pltpu.make_async_copy(
                y_hbm.at[pl.ds(pos_ref[c, r], 1)], buf_ref.at[c, pl.ds(r, 1)], sem).start()
        return carry

    lax.fori_loop(0, tb, issue, 0)
    for c in range(TOP_K):
        pltpu.make_async_copy(y_hbm.at[pl.ds(0, tb)], buf_ref.at[c], sem).wait()
    w = w_ref[...]
    h = h_ref[...] + (w[:, 0:1] * buf_ref[0] + w[:, 1:2] * buf_ref[1])
    y = h * lax.rsqrt(jnp.mean(h * h, axis=-1, keepdims=True) + RMS_EPS)
    o_ref[...] = (y * g_ref[...]).astype(o_ref.dtype)


def _combine_norm(pos, y_sorted, h, gate_w, gain, out_dtype, tb=256):
    t, d = h.shape
    tb = min(tb, t)
    return pl.pallas_call(
        functools.partial(_combine_kernel, tb=tb),
        grid=(t // tb,),
        in_specs=[
            pl.BlockSpec((TOP_K, tb), lambda i: (0, i), memory_space=pltpu.SMEM),
            pl.BlockSpec(memory_space=pl.ANY),
            pl.BlockSpec((tb, d), lambda i: (i, 0)),
            pl.BlockSpec((tb, TOP_K), lambda i: (i, 0)),
            pl.BlockSpec((1, d), lambda i: (0, 0)),
        ],
        out_specs=pl.BlockSpec((tb, d), lambda i: (i, 0)),
        out_shape=jax.ShapeDtypeStruct((t, d), out_dtype),
        scratch_shapes=[pltpu.VMEM((TOP_K, tb, d), F32), pltpu.SemaphoreType.DMA(())],
        compiler_params=_params("arbitrary"),
        name="moe_combine_norm",
    )(pos, y_sorted, h, gate_w, gain.reshape(1, d).astype(F32))


def kernel(x, norm_mix, norm_ffn, w_qkv_a, w_o_a, norm_kv, w_kvf_b, b_f, w_q_b, w_o_b,
           w_gu_dense, w_down_dense, w_router, w_gu_exp, w_down_exp, norm_final):
    batch, seq, d_model = x.shape
    n_heads = b_f.shape[0]
    d_attn = w_o_a.shape[1]
    head_dim = d_attn // n_heads
    n_experts = w_router.shape[-1]
    assert TOP_K == 2 and norm_mix.shape[0] == 2, "kernel is written for the depth-2 trunk"
    t = batch * seq
    h = x.reshape(t, d_model)

    (hn,) = _rmsnorm(h, [norm_mix[0]], BF16)
    qkv = _matmul(hn, w_qkv_a, (0,), 3 * d_attn, BF16, name="mm_qkv_a")
    o = _sb_attention(qkv, batch, seq, n_heads, head_dim)
    h = _matmul(o, w_o_a, (0,), d_model, F32, residual=h, name="mm_o_a")
    (hn,) = _rmsnorm(h, [norm_ffn[0]], BF16)
    act = _matmul_swiglu(hn, w_gu_dense, (0,), name="mm_gu_dense")
    h = _matmul_down(act, w_down_dense, (0,), h, name="mm_down_dense")

    w_f_t = w_kvf_b[:, 2 * d_attn:].T
    hn_q, hn_kv, flogit_t = _rmsnorm(h, [norm_mix[1], norm_kv], BF16, proj_t=w_f_t)
    kv = _matmul(hn_kv, w_kvf_b, (), 2 * d_attn, BF16, name="mm_kv_b")
    q = _matmul(hn_q, w_q_b, (0,), d_attn, BF16, name="mm_q_b")
    cum_log_f = _fgate_cumsum(flogit_t, b_f, batch, seq)
    o = _fox_attention(q, kv, cum_log_f, batch, seq, n_heads, head_dim)
    h = _matmul(o, w_o_b, (0,), d_model, F32, residual=h, name="mm_o_b")

    hn, logits_t = _rmsnorm(h, [norm_ffn[1]], BF16, proj_t=w_router[0].T)
    tm = min(MOE_ROW_TILE, t)
    m_pad = TOP_K * t + n_experts * tm
    gate_w, pos, meta = _route(logits_t, tm)
    blk_expert, n_used = meta[0, :m_pad // tm], meta[1, :1]
    x_sorted = _dispatch(pos, hn, m_pad)
    act = _grouped_swiglu(blk_expert, n_used, x_sorted, w_gu_exp, 0, tm)
    y_sorted = _grouped_down(blk_expert, n_used, act, w_down_exp, 0, tm)
    out = _combine_norm(pos, y_sorted, h, gate_w.T, norm_final, x.dtype)
    return out.reshape(batch, seq, d_model)
```

```python
import functools

import jax
import jax.numpy as jnp
from jax import lax
from jax.experimental import pallas as pl
from jax.experimental.pallas import tpu as pltpu

RMS_EPS = 1e-6
TOP_K = 2

V7X_VMEM_LIMIT_BYTES = 56 * 1024 * 1024

VREG_SUBLANES = 8
VREG_LANES = 128

MOE_ROW_TILE = 512
ATTN_TQ = 512
ATTN_TK = 256

BF16 = jnp.bfloat16
F32 = jnp.float32


def _params(*semantics):
    return pltpu.CompilerParams(
        dimension_semantics=semantics, vmem_limit_bytes=V7X_VMEM_LIMIT_BYTES)


def _dot(a, b):
    return jnp.dot(a, b, preferred_element_type=F32)


def _dot_nt(a, b):
    return lax.dot_general(a, b, (((1,), (1,)), ((), ())), preferred_element_type=F32)


def _split3(x):
    hi = x.astype(BF16)
    r1 = x - hi.astype(F32)
    mid = r1.astype(BF16)
    lo = (r1 - mid.astype(F32)).astype(BF16)
    return hi, mid, lo


def _log_sigmoid(z):
    return jnp.minimum(z, 0.0) - jnp.log(1.0 + jnp.exp(-jnp.abs(z)))


def _rmsnorm_kernel(*refs, n_gains, has_proj):
    x_ref = refs[0]
    g_refs = refs[1:1 + n_gains]
    pos = 1 + n_gains
    p_ref = refs[pos] if has_proj else None
    pos += int(has_proj)
    o_refs = refs[pos:pos + n_gains]
    po_ref = refs[pos + n_gains] if has_proj else None

    x = x_ref[...]
    y = x * lax.rsqrt(jnp.mean(x * x, axis=-1, keepdims=True) + RMS_EPS)
    for g_ref, o_ref in zip(g_refs, o_refs):
        o_ref[...] = (y * g_ref[...]).astype(o_ref.dtype)
    if has_proj:
        yn = y * g_refs[-1][...]
        yh, ym, _ = _split3(yn)
        ph, pm, _ = _split3(p_ref[...])
        po_ref[...] = _dot_nt(ph, yh) + (_dot_nt(ph, ym) + _dot_nt(pm, yh))


def _rmsnorm(x, gains, out_dtype, proj_t=None, tm=256):
    t, d = x.shape
    tm = min(tm, t)
    n_gains = len(gains)
    has_proj = proj_t is not None
    in_specs = [pl.BlockSpec((tm, d), lambda i: (i, 0))]
    in_specs += [pl.BlockSpec((1, d), lambda i: (0, 0))] * n_gains
    args = [x] + [g.reshape(1, d).astype(F32) for g in gains]
    out_shape = [jax.ShapeDtypeStruct((t, d), out_dtype)] * n_gains
    out_specs = [pl.BlockSpec((tm, d), lambda i: (i, 0))] * n_gains
    if has_proj:
        n = proj_t.shape[0]
        in_specs.append(pl.BlockSpec((n, d), lambda i: (0, 0)))
        args.append(proj_t)
        out_shape.append(jax.ShapeDtypeStruct((n, t), F32))
        out_specs.append(pl.BlockSpec((n, tm), lambda i: (0, i)))
    outs = pl.pallas_call(
        functools.partial(_rmsnorm_kernel, n_gains=n_gains, has_proj=has_proj),
        grid=(t // tm,),
        in_specs=in_specs,
        out_specs=out_specs,
        out_shape=out_shape,
        compiler_params=_params("parallel"),
        name="rmsnorm",
    )(*args)
    return outs


def _mm_plain_kernel(a_ref, w_ref, o_ref):
    o_ref[...] = _dot(a_ref[...], w_ref[...].astype(BF16)).astype(o_ref.dtype)


def _mm_res_kernel(a_ref, w_ref, r_ref, o_ref):
    o_ref[...] = r_ref[...] + _dot(a_ref[...], w_ref[...].astype(BF16))


def _mm_swiglu_kernel(a_ref, wg_ref, wu_ref, o_ref):
    a = a_ref[...]
    g = _dot(a, wg_ref[...].astype(BF16))
    u = _dot(a, wu_ref[...].astype(BF16))
    o_ref[...] = (g * jax.nn.sigmoid(g) * u).astype(o_ref.dtype)


def _w_spec(w, k, tn, prefix, col_block_offset=0):
    lead = (None,) * len(prefix)
    return pl.BlockSpec(
        lead + (k, tn), lambda n, m: tuple(prefix) + (0, n + col_block_offset))


def _matmul(a, w, prefix, n_out, out_dtype, residual=None, tm=1024, tn=1024, name="mm"):
    m_dim, k = a.shape
    tm, tn = min(tm, m_dim), min(tn, n_out)
    grid = (n_out // tn, m_dim // tm)
    in_specs = [pl.BlockSpec((tm, k), lambda n, m: (m, 0)), _w_spec(w, k, tn, prefix)]
    args = [a, w]
    kernel = _mm_plain_kernel
    if residual is not None:
        in_specs.append(pl.BlockSpec((tm, tn), lambda n, m: (m, n)))
        args.append(residual)
        kernel = _mm_res_kernel
    return pl.pallas_call(
        kernel,
        grid=grid,
        in_specs=in_specs,
        out_specs=pl.BlockSpec((tm, tn), lambda n, m: (m, n)),
        out_shape=jax.ShapeDtypeStruct((m_dim, n_out), out_dtype),
        compiler_params=_params("parallel", "parallel"),
        name=name,
    )(*args)


def _matmul_swiglu(a, w_gu, prefix, tm=1024, tn=512, name="mm_swiglu"):
    m_dim, k = a.shape
    f = w_gu.shape[-1] // 2
    tm, tn = min(tm, m_dim), min(tn, f)
    grid = (f // tn, m_dim // tm)
    return pl.pallas_call(
        _mm_swiglu_kernel,
        grid=grid,
        in_specs=[
            pl.BlockSpec((tm, k), lambda n, m: (m, 0)),
            _w_spec(w_gu, k, tn, prefix),
            _w_spec(w_gu, k, tn, prefix, col_block_offset=f // tn),
        ],
        out_specs=pl.BlockSpec((tm, tn), lambda n, m: (m, n)),
        out_shape=jax.ShapeDtypeStruct((m_dim, f), BF16),
        compiler_params=_params("parallel", "parallel"),
        name=name,
    )(a, w_gu, w_gu)


def _mm_down_kernel(a_ref, w_ref, r_ref, o_ref):
    @pl.when(pl.program_id(1) == 0)
    def _():
        o_ref[...] = r_ref[...]

    o_ref[...] += _dot(a_ref[...], w_ref[...].astype(BF16))


def _matmul_down(a, w, prefix, residual, tm=1024, tk=512, name="mm_down"):
    m_dim, k_dim = a.shape
    n = w.shape[-1]
    tm, tk = min(tm, m_dim), min(tk, k_dim)
    nk = k_dim // tk
    lead = (None,) * len(prefix)
    return pl.pallas_call(
        _mm_down_kernel,
        grid=(m_dim // tm, nk),
        in_specs=[
            pl.BlockSpec((tm, tk), lambda m, k: (m, k)),
            pl.BlockSpec(lead + (tk, n), lambda m, k: tuple(prefix) + (k, 0)),
            pl.BlockSpec((tm, n), lambda m, k: (m, 0)),
        ],
        out_specs=pl.BlockSpec((tm, n), lambda m, k: (m, 0)),
        out_shape=jax.ShapeDtypeStruct((m_dim, n), F32),
        compiler_params=_params("parallel", "arbitrary"),
        name=name,
    )(a, w, residual)


def _strict_lower_ones(t):
    j = lax.broadcasted_iota(jnp.int32, (t, t), 0)
    s = lax.broadcasted_iota(jnp.int32, (t, t), 1)
    return jnp.where(j > s, 1.0, 0.0).astype(BF16)


def _sb_attn_kernel(q_ref, k_ref, v_ref, o_ref, acc_ref, run_ref, ls_ref, hi_ref, lo_ref,
                    rs_ref, *, tq, tk, scale):
    i = pl.program_id(2)
    r = tq // tk
    tri = _strict_lower_ones(tk)
    row = lax.broadcasted_iota(jnp.int32, (tq, tk), 0)
    col = lax.broadcasted_iota(jnp.int32, (tq, tk), 1)

    def score_stage(j, mask, slot):
        start = pl.multiple_of(j * tk, tk)
        z = _dot_nt(q_ref[...], k_ref[pl.ds(start, tk), :]) * scale
        ls = _log_sigmoid(z)
        l1m = ls - z
        if mask is not None:
            l1m = jnp.where(mask, l1m, 0.0)
            ls = jnp.where(mask, ls, -jnp.inf)
        hi = l1m.astype(BF16)
        ls_ref[slot] = ls
        hi_ref[slot] = hi
        lo_ref[slot] = (l1m - hi.astype(F32)).astype(BF16)
        rs_ref[slot] = jnp.sum(l1m, axis=-1, keepdims=True)

    def value_stage(j, slot):
        start = pl.multiple_of(j * tk, tk)
        suffix = _dot(hi_ref[slot], tri) + _dot(lo_ref[slot], tri)
        w = jnp.exp(ls_ref[slot] + suffix + run_ref[...])
        acc_ref[...] += _dot(w.astype(BF16), v_ref[pl.ds(start, tk), :])
        run_ref[...] += rs_ref[slot]

    acc_ref[...] = jnp.zeros_like(acc_ref)
    run_ref[...] = jnp.zeros_like(run_ref)
    prev = None
    for d in reversed(range(r)):
        score_stage(i * r + d, col + d * tk < row, d)
        if prev is not None:
            value_stage(*prev)
        prev = (i * r + d, d)

    def body(t, carry):
        base = (i - 1 - t) * r
        pending = (base + r, 0)
        for d in reversed(range(r)):
            score_stage(base + d, None, d)
            value_stage(*pending)
            pending = (base + d, d)
        return carry

    lax.fori_loop(0, i, body, 0)
    value_stage(0, 0)
    o_ref[...] = acc_ref[...].astype(o_ref.dtype)


def _sb_attention(qkv, batch, seq, n_heads, head_dim, tq=ATTN_TQ, tk=ATTN_TK):
    tq, tk = min(tq, seq), min(tk, seq)
    nq = seq // tq
    r = tq // tk
    return pl.pallas_call(
        functools.partial(_sb_attn_kernel, tq=tq, tk=tk, scale=head_dim ** -0.5),
        grid=(batch, n_heads, nq),
        in_specs=[
            pl.BlockSpec((tq, head_dim), lambda b, h, i: (b * nq + i, h)),
            pl.BlockSpec((seq, head_dim), lambda b, h, i: (b, n_heads + h)),
            pl.BlockSpec((seq, head_dim), lambda b, h, i: (b, 2 * n_heads + h)),
        ],
        out_specs=pl.BlockSpec((tq, head_dim), lambda b, h, i: (b * nq + i, h)),
        out_shape=jax.ShapeDtypeStruct((batch * seq, n_heads * head_dim), BF16),
        scratch_shapes=[
            pltpu.VMEM((tq, head_dim), F32),
            pltpu.VMEM((tq, 1), F32),
            pltpu.VMEM((r, tq, tk), F32),
            pltpu.VMEM((r, tq, tk), BF16),
            pltpu.VMEM((r, tq, tk), BF16),
            pltpu.VMEM((r, tq, 1), F32),
        ],
        compiler_params=_params("parallel", "parallel", "parallel"),
        name="sb_attention",
    )(qkv, qkv, qkv)


def _fgate_cumsum_kernel(fl_ref, bf_ref, c_ref, *, seq):
    x = _log_sigmoid(fl_ref[...] + bf_ref[...])
    j = lax.broadcasted_iota(jnp.int32, (seq, seq), 0)
    t = lax.broadcasted_iota(jnp.int32, (seq, seq), 1)
    upper = jnp.where(j <= t, 1.0, 0.0).astype(BF16)
    hi, mid, lo = _split3(x)
    c_ref[...] = _dot(hi, upper) + (_dot(mid, upper) + _dot(lo, upper))


def _fgate_cumsum(flogit_t, b_f, batch, seq):
    n_heads = flogit_t.shape[0]
    return pl.pallas_call(
        functools.partial(_fgate_cumsum_kernel, seq=seq),
        grid=(batch,),
        in_specs=[
            pl.BlockSpec((n_heads, seq), lambda b: (0, b)),
            pl.BlockSpec((n_heads, 1), lambda b: (0, 0)),
        ],
        out_specs=pl.BlockSpec((None, n_heads, seq), lambda b: (b, 0, 0)),
        out_shape=jax.ShapeDtypeStruct((batch, n_heads, seq), F32),
        compiler_params=_params("parallel"),
        name="fgate_cumsum",
    )(flogit_t, b_f.reshape(n_heads, 1).astype(F32))


def _fox_attn_kernel(q_ref, k_ref, v_ref, cq_ref, ck_ref, o_ref, acc_ref, bias_ref, red_ref,
                     p_ref, *, tq, tk, scale):
    i = pl.program_id(2)
    r = tq // tk
    row = lax.broadcasted_iota(jnp.int32, (tq, tk), 0)
    col = lax.broadcasted_iota(jnp.int32, (tq, tk), 1)
    band_masks = [col + d * tk <= row for d in range(r)]

    def logits(j, mask):
        start = pl.multiple_of(j * tk, tk)
        s = (_dot_nt(q_ref[...], k_ref[pl.ds(start, tk), :]) * scale
             + bias_ref[...] - ck_ref[:, pl.ds(start, tk)])
        if mask is not None:
            s = jnp.where(mask, s, -jnp.inf)
        return s

    bias_ref[...] = jnp.broadcast_to(cq_ref[...], (tq, tk))
    red_ref[...] = jnp.full((tq, tk), -jnp.inf, F32)

    def max_step(j, mask):
        red_ref[...] = jnp.maximum(red_ref[...], logits(j, mask))

    for d in range(r):
        max_step(i * r + d, band_masks[d])

    def max_body(t, carry):
        for d in range(r):
            max_step(t * r + d, None)
        return carry

    lax.fori_loop(0, i, max_body, 0)
    row_max = jnp.max(red_ref[...], axis=-1, keepdims=True)

    bias_ref[...] = jnp.broadcast_to(cq_ref[...] - row_max, (tq, tk))
    red_ref[...] = jnp.zeros((tq, tk), F32)
    acc_ref[...] = jnp.zeros_like(acc_ref)

    def prob_stage(j, mask, slot):
        p = jnp.exp(logits(j, mask))
        red_ref[...] += p
        p_ref[slot] = p.astype(BF16)

    def value_stage(j, slot):
        start = pl.multiple_of(j * tk, tk)
        acc_ref[...] += _dot(p_ref[slot], v_ref[pl.ds(start, tk), :])

    prev = None
    for d in range(r):
        prob_stage(i * r + d, band_masks[d], d)
        if prev is not None:
            value_stage(*prev)
        prev = (i * r + d, d)

    def pv_body(t, carry):
        base = t * r
        pending = (jnp.where(t == 0, (i + 1) * r - 1, base - 1), r - 1)
        for d in range(r):
            prob_stage(base + d, None, d)
            value_stage(*pending)
            pending = (base + d, d)
        return carry

    lax.fori_loop(0, i, pv_body, 0)
    value_stage(jnp.where(i == 0, r - 1, i * r - 1), r - 1)
    denom = jnp.sum(red_ref[...], axis=-1, keepdims=True)
    o_ref[...] = (acc_ref[...] / denom).astype(o_ref.dtype)


def _fox_attention(q, kv, cum_log_f, batch, seq, n_heads, head_dim, tq=ATTN_TQ, tk=ATTN_TK):
    tq, tk = min(tq, seq), min(tk, seq)
    nq = seq // tq
    r = tq // tk
    cq = cum_log_f.reshape(batch, n_heads, seq, 1)
    ck = cum_log_f.reshape(batch, n_heads, 1, seq)
    return pl.pallas_call(
        functools.partial(_fox_attn_kernel, tq=tq, tk=tk, scale=head_dim ** -0.5),
        grid=(batch, n_heads, nq),
        in_specs=[
            pl.BlockSpec((tq, head_dim), lambda b, h, i: (b * nq + i, h)),
            pl.BlockSpec((seq, head_dim), lambda b, h, i: (b, h)),
            pl.BlockSpec((seq, head_dim), lambda b, h, i: (b, n_heads + h)),
            pl.BlockSpec((None, None, tq, 1), lambda b, h, i: (b, h, i, 0)),
            pl.BlockSpec((None, None, 1, seq), lambda b, h, i: (b, h, 0, 0)),
        ],
        out_specs=pl.BlockSpec((tq, head_dim), lambda b, h, i: (b * nq + i, h)),
        out_shape=jax.ShapeDtypeStruct((batch * seq, n_heads * head_dim), BF16),
        scratch_shapes=[
            pltpu.VMEM((tq, head_dim), F32),
            pltpu.VMEM((tq, tk), F32),
            pltpu.VMEM((tq, tk), F32),
            pltpu.VMEM((r, tq, tk), BF16),
        ],
        compiler_params=_params("parallel", "parallel", "parallel"),
        name="fox_attention",
    )(q, kv, kv, cq, ck)


def _route_kernel(l_ref, w_ref, pos_ref, meta_ref, *, tm, chunk):
    l = l_ref[...]
    n_exp, t = l.shape
    e_idx = lax.broadcasted_iota(jnp.int32, l.shape, 0)
    m1 = jnp.max(l, axis=0, keepdims=True)
    i1 = jnp.min(jnp.where(l == m1, e_idx, n_exp), axis=0, keepdims=True)
    sel1 = e_idx == i1
    l2 = jnp.where(sel1, -jnp.inf, l)
    m2 = jnp.max(l2, axis=0, keepdims=True)
    i2 = jnp.min(jnp.where(l2 == m2, e_idx, n_exp), axis=0, keepdims=True)
    sel2 = e_idx == i2
    e2 = jnp.exp(m2 - m1)
    denom = 1.0 + e2
    w_ref[0:1, :] = 1.0 / denom
    w_ref[1:2, :] = e2 / denom

    s1 = jnp.where(sel1, 1.0, 0.0)
    s2 = jnp.where(sel2, 1.0, 0.0)
    s_any = (s1 + s2).astype(BF16)
    cnt_col = jnp.sum(s1 + s2, axis=1, keepdims=True)
    cnt_row = _dot_nt(jnp.ones((n_exp, t), BF16), s_any)

    def padded(c):
        return jnp.floor((c + (tm - 1)) * (1.0 / tm)) * tm

    ei = lax.broadcasted_iota(jnp.int32, (n_exp, n_exp), 0)
    ej = lax.broadcasted_iota(jnp.int32, (n_exp, n_exp), 1)
    start_col = jnp.sum(jnp.where(ej < ei, padded(cnt_row), 0.0), axis=1, keepdims=True)
    end_col = start_col + padded(cnt_col)

    jj = lax.broadcasted_iota(jnp.int32, (chunk, chunk), 0)
    tt = lax.broadcasted_iota(jnp.int32, (chunk, chunk), 1)
    before = jnp.where(jj < tt, 1.0, 0.0).astype(BF16)
    carry = jnp.zeros((n_exp, 1), F32)
    for c in range(t // chunk):
        sl = slice(c * chunk, (c + 1) * chunk)
        dest = start_col + carry + _dot(s_any[:, sl], before)
        pos_ref[0:1, sl] = jnp.sum(s1[:, sl] * dest, axis=0, keepdims=True).astype(jnp.int32)
        pos_ref[1:2, sl] = jnp.sum(s2[:, sl] * dest, axis=0, keepdims=True).astype(jnp.int32)
        carry = carry + jnp.sum(s1[:, sl] + s2[:, sl], axis=1, keepdims=True)

    lanes = meta_ref.shape[1]
    blk_start = lax.broadcasted_iota(jnp.int32, (n_exp, lanes), 1).astype(F32) * tm
    blk_expert = jnp.sum(jnp.where(end_col <= blk_start, 1.0, 0.0), axis=0, keepdims=True)
    blk_expert = jnp.minimum(blk_expert, n_exp - 1.0)
    n_used = jnp.max(end_col, axis=0, keepdims=True) * (1.0 / tm)
    meta_row = lax.broadcasted_iota(jnp.int32, meta_ref.shape, 0)
    meta_ref[...] = jnp.where(meta_row == 0, blk_expert, n_used).astype(jnp.int32)


def _route(logits_t, tm):
    n_exp, t = logits_t.shape
    assert tm & (tm - 1) == 0, "row tile must be a power of two"
    assert TOP_K * t // tm + n_exp <= VREG_LANES
    chunk = min(1024, t)
    return pl.pallas_call(
        functools.partial(_route_kernel, tm=tm, chunk=chunk),
        out_shape=[
            jax.ShapeDtypeStruct((TOP_K, t), F32),
            jax.ShapeDtypeStruct((TOP_K, t), jnp.int32),
            jax.ShapeDtypeStruct((VREG_SUBLANES, VREG_LANES), jnp.int32),
        ],
        compiler_params=pltpu.CompilerParams(vmem_limit_bytes=V7X_VMEM_LIMIT_BYTES),
        name="moe_route",
    )(logits_t)


def _dispatch_kernel(pos_ref, x_ref, init_hbm, o_hbm, sem, *, tb):
    del init_hbm

    def issue(r, carry):
        for c in range(TOP_K):
            pltpu.make_async_copy(x_ref.at[r], o_hbm.at[pos_ref[c, r]], sem).start()
        return carry

    lax.fori_loop(0, tb, issue, 0)
    for c in range(TOP_K):
        pltpu.make_async_copy(x_ref, o_hbm.at[pl.ds(0, tb)], sem).wait()


def _dispatch(pos, x, m_pad, tb=256):
    t, d = x.shape
    tb = min(tb, t)
    x3 = x.reshape(t, d // VREG_LANES, VREG_LANES)
    out = pl.pallas_call(
        functools.partial(_dispatch_kernel, tb=tb),
        grid=(t // tb,),
        in_specs=[
            pl.BlockSpec((TOP_K, tb), lambda i: (0, i), memory_space=pltpu.SMEM),
            pl.BlockSpec((tb,) + x3.shape[1:], lambda i: (i, 0, 0)),
            pl.BlockSpec(memory_space=pl.ANY),
        ],
        out_specs=pl.BlockSpec(memory_space=pl.ANY),
        out_shape=jax.ShapeDtypeStruct((m_pad,) + x3.shape[1:], x.dtype),
        scratch_shapes=[pltpu.SemaphoreType.DMA(())],
        input_output_aliases={2: 0},
        compiler_params=_params("arbitrary"),
        name="moe_dispatch",
    )(pos, x3, jnp.zeros((m_pad,) + x3.shape[1:], x.dtype))
    return out.reshape(m_pad, d)


def _clamp_block(b, n_used_ref):
    return jnp.minimum(b, n_used_ref[0] - 1)


def _gmm_swiglu_kernel(be_ref, nu_ref, a_ref, wg_ref, wu_ref, wd_ref, o_ref, wd_bf16_ref):
    del be_ref
    wd_bf16_ref[...] = wd_ref[...].astype(BF16)

    used = pl.program_id(1) < nu_ref[0]

    @pl.when(used)
    def _():
        _mm_swiglu_kernel(a_ref, wg_ref, wu_ref, o_ref)

    @pl.when(jnp.logical_not(used))
    def _():
        o_ref[...] = jnp.zeros_like(o_ref)


def _cast_chunk_rows(total_rows, n_steps):
    bf16_tile_rows = 2 * VREG_SUBLANES
    for rows in range(bf16_tile_rows, total_rows + 1, bf16_tile_rows):
        if total_rows % rows == 0 and total_rows // rows <= n_steps:
            return rows
    raise ValueError("no chunking of the down weights fits the grid")


def _grouped_swiglu(blk_expert, n_used, a, w_gu, w_down, layer, tm, tn=1024):
    m_pad, k = a.shape
    f = w_gu.shape[-1] // 2
    tn = min(tn, f)
    nb = m_pad // tm
    rows, d = w_down.shape
    chunk = _cast_chunk_rows(rows, (f // tn) * nb)
    last_chunk = rows // chunk - 1

    def a_map(n, b, be, nu):
        return (_clamp_block(b, nu), 0)

    def w_map(off):
        return lambda n, b, be, nu: (layer, be[_clamp_block(b, nu)], 0, n + off)

    def chunk_map(n, b, be, nu):
        return (jnp.minimum(n * nb + b, last_chunk), 0)

    return pl.pallas_call(
        _gmm_swiglu_kernel,
        grid_spec=pltpu.PrefetchScalarGridSpec(
            num_scalar_prefetch=2,
            grid=(f // tn, nb),
            in_specs=[
                pl.BlockSpec((tm, k), a_map),
                pl.BlockSpec((None, None, k, tn), w_map(0)),
                pl.BlockSpec((None, None, k, tn), w_map(f // tn)),
                pl.BlockSpec((chunk, d), chunk_map),
            ],
            out_specs=[
                pl.BlockSpec((tm, tn), lambda n, b, be, nu: (b, n)),
                pl.BlockSpec((chunk, d), chunk_map),
            ],
        ),
        out_shape=[
            jax.ShapeDtypeStruct((m_pad, f), BF16),
            jax.ShapeDtypeStruct((rows, d), BF16),
        ],
        compiler_params=_params("arbitrary", "arbitrary"),
        name="moe_gu",
    )(blk_expert, n_used, a, w_gu, w_gu, w_down)


def _gmm_down_kernel(be_ref, nu_ref, a_ref, w_ref, o_ref):
    del be_ref
    k = pl.program_id(1)

    @pl.when(k == 0)
    def _():
        o_ref[...] = jnp.zeros_like(o_ref)

    @pl.when(pl.program_id(0) < nu_ref[0])
    def _():
        o_ref[...] += _dot(a_ref[...], w_ref[...])


def _grouped_down(blk_expert, n_used, a, w_down_bf16, tm, tk=1024):
    m_pad, k_dim = a.shape
    n = w_down_bf16.shape[-1]
    tk = min(tk, k_dim)
    nk = k_dim // tk
    nb = m_pad // tm

    def k_eff(b, k, nu):
        return jnp.where(b < nu[0], k, nk - 1)

    return pl.pallas_call(
        _gmm_down_kernel,
        grid_spec=pltpu.PrefetchScalarGridSpec(
            num_scalar_prefetch=2,
            grid=(nb, nk),
            in_specs=[
                pl.BlockSpec((tm, tk), lambda b, k, be, nu: (_clamp_block(b, nu), k_eff(b, k, nu))),
                pl.BlockSpec((tk, n),
                             lambda b, k, be, nu: (be[_clamp_block(b, nu)] * nk + k_eff(b, k, nu),
                                                   0)),
            ],
            out_specs=pl.BlockSpec((tm, n), lambda b, k, be, nu: (b, 0)),
        ),
        out_shape=jax.ShapeDtypeStruct((m_pad, n), F32),
        compiler_params=_params("arbitrary", "arbitrary"),
        name="moe_down",
    )(blk_expert, n_used, a, w_down_bf16)


def _combine_kernel(pos_ref, y_hbm, h_ref, w_ref, g_ref, o_ref, buf_ref, sem, *, tb):
    def issue(r, carry):
        for c in range(TOP_K):
            pltpu.make_async_copy(
                y_hbm.at[pl.ds(pos_ref[c, r], 1)], buf_ref.at[c, pl.ds(r, 1)], sem).start()
        return carry

    lax.fori_loop(0, tb, issue, 0)
    for c in range(TOP_K):
        pltpu.make_async_copy(y_hbm.at[pl.ds(0, tb)], buf_ref.at[c], sem).wait()
    w = w_ref[...]
    h = h_ref[...] + (w[:, 0:1] * buf_ref[0] + w[:, 1:2] * buf_ref[1])
    y = h * lax.rsqrt(jnp.mean(h * h, axis=-1, keepdims=True) + RMS_EPS)
    o_ref[...] = (y * g_ref[...]).astype(o_ref.dtype)


def _combine_norm(pos, y_sorted, h, gate_w, gain, out_dtype, tb=256):
    t, d = h.shape
    tb = min(tb, t)
    return pl.pallas_call(
        functools.partial(_combine_kernel, tb=tb),
        grid=(t // tb,),
        in_specs=[
            pl.BlockSpec((TOP_K, tb), lambda i: (0, i), memory_space=pltpu.SMEM),
            pl.BlockSpec(memory_space=pl.ANY),
            pl.BlockSpec((tb, d), lambda i: (i, 0)),
            pl.BlockSpec((tb, TOP_K), lambda i: (i, 0)),
            pl.BlockSpec((1, d), lambda i: (0, 0)),
        ],
        out_specs=pl.BlockSpec((tb, d), lambda i: (i, 0)),
        out_shape=jax.ShapeDtypeStruct((t, d), out_dtype),
        scratch_shapes=[pltpu.VMEM((TOP_K, tb, d), F32), pltpu.SemaphoreType.DMA(())],
        compiler_params=_params("arbitrary"),
        name="moe_combine_norm",
    )(pos, y_sorted, h, gate_w, gain.reshape(1, d).astype(F32))


def kernel(x, norm_mix, norm_ffn, w_qkv_a, w_o_a, norm_kv, w_kvf_b, b_f, w_q_b, w_o_b,
           w_gu_dense, w_down_dense, w_router, w_gu_exp, w_down_exp, norm_final):
    batch, seq, d_model = x.shape
    n_heads = b_f.shape[0]
    d_attn = w_o_a.shape[1]
    head_dim = d_attn // n_heads
    n_experts = w_router.shape[-1]
    assert TOP_K == 2 and norm_mix.shape[0] == 2, "kernel is written for the depth-2 trunk"
    t = batch * seq
    h = x.reshape(t, d_model)

    (hn,) = _rmsnorm(h, [norm_mix[0]], BF16)
    qkv = _matmul(hn, w_qkv_a, (0,), 3 * d_attn, BF16, name="mm_qkv_a")
    o = _sb_attention(qkv, batch, seq, n_heads, head_dim)
    h = _matmul(o, w_o_a, (0,), d_model, F32, residual=h, name="mm_o_a")
    (hn,) = _rmsnorm(h, [norm_ffn[0]], BF16)
    act = _matmul_swiglu(hn, w_gu_dense, (0,), name="mm_gu_dense")
    h = _matmul_down(act, w_down_dense, (0,), h, name="mm_down_dense")

    w_f_t = w_kvf_b[:, 2 * d_attn:].T
    hn_q, hn_kv, flogit_t = _rmsnorm(h, [norm_mix[1], norm_kv], BF16, proj_t=w_f_t)
    kv = _matmul(hn_kv, w_kvf_b, (), 2 * d_attn, BF16, name="mm_kv_b")
    q = _matmul(hn_q, w_q_b, (0,), d_attn, BF16, name="mm_q_b")
    cum_log_f = _fgate_cumsum(flogit_t, b_f, batch, seq)
    o = _fox_attention(q, kv, cum_log_f, batch, seq, n_heads, head_dim)
    h = _matmul(o, w_o_b, (0,), d_model, F32, residual=h, name="mm_o_b")

    hn, logits_t = _rmsnorm(h, [norm_ffn[1]], BF16, proj_t=w_router[0].T)
    tm = min(MOE_ROW_TILE, t)
    m_pad = TOP_K * t + n_experts * tm
    gate_w, pos, meta = _route(logits_t, tm)
    blk_expert, n_used = meta[0, :m_pad // tm], meta[1, :1]
    x_sorted = _dispatch(pos, hn, m_pad)
    w_down = w_down_exp[0].reshape(-1, d_model)
    act, w_down_bf16 = _grouped_swiglu(blk_expert, n_used, x_sorted, w_gu_exp, w_down, 0, tm)
    y_sorted = _grouped_down(blk_expert, n_used, act, w_down_bf16, tm)
    out = _combine_norm(pos, y_sorted, h, gate_w.T, norm_final, x.dtype)
    return out.reshape(batch, seq, d_model)
```

```python
import functools

import jax
import jax.numpy as jnp
from jax import lax
from jax.experimental import pallas as pl
from jax.experimental.pallas import tpu as pltpu

RMS_EPS = 1e-6
TOP_K = 2

V7X_VMEM_LIMIT_BYTES = 56 * 1024 * 1024

VREG_SUBLANES = 8
VREG_LANES = 128

MOE_ROW_TILE = 512
ATTN_TQ = 512
ATTN_TK = 256

BF16 = jnp.bfloat16
F32 = jnp.float32


def _params(*semantics):
    return pltpu.CompilerParams(
        dimension_semantics=semantics, vmem_limit_bytes=V7X_VMEM_LIMIT_BYTES)


def _dot(a, b):
    return jnp.dot(a, b, preferred_element_type=F32)


def _dot_nt(a, b):
    return lax.dot_general(a, b, (((1,), (1,)), ((), ())), preferred_element_type=F32)


def _split3(x):
    hi = x.astype(BF16)
    r1 = x - hi.astype(F32)
    mid = r1.astype(BF16)
    lo = (r1 - mid.astype(F32)).astype(BF16)
    return hi, mid, lo


def _log_sigmoid(z):
    return jnp.minimum(z, 0.0) - jnp.log(1.0 + jnp.exp(-jnp.abs(z)))


def _rmsnorm_kernel(*refs, n_gains, has_proj):
    x_ref = refs[0]
    g_refs = refs[1:1 + n_gains]
    pos = 1 + n_gains
    p_ref = refs[pos] if has_proj else None
    pos += int(has_proj)
    o_refs = refs[pos:pos + n_gains]
    po_ref = refs[pos + n_gains] if has_proj else None

    x = x_ref[...]
    y = x * lax.rsqrt(jnp.mean(x * x, axis=-1, keepdims=True) + RMS_EPS)
    for g_ref, o_ref in zip(g_refs, o_refs):
        o_ref[...] = (y * g_ref[...]).astype(o_ref.dtype)
    if has_proj:
        yn = y * g_refs[-1][...]
        yh, ym, _ = _split3(yn)
        ph, pm, _ = _split3(p_ref[...])
        po_ref[...] = _dot_nt(ph, yh) + (_dot_nt(ph, ym) + _dot_nt(pm, yh))


def _rmsnorm(x, gains, out_dtype, proj_t=None, tm=256):
    t, d = x.shape
    tm = min(tm, t)
    n_gains = len(gains)
    has_proj = proj_t is not None
    in_specs = [pl.BlockSpec((tm, d), lambda i: (i, 0))]
    in_specs += [pl.BlockSpec((1, d), lambda i: (0, 0))] * n_gains
    args = [x] + [g.reshape(1, d).astype(F32) for g in gains]
    out_shape = [jax.ShapeDtypeStruct((t, d), out_dtype)] * n_gains
    out_specs = [pl.BlockSpec((tm, d), lambda i: (i, 0))] * n_gains
    if has_proj:
        n = proj_t.shape[0]
        in_specs.append(pl.BlockSpec((n, d), lambda i: (0, 0)))
        args.append(proj_t)
        out_shape.append(jax.ShapeDtypeStruct((n, t), F32))
        out_specs.append(pl.BlockSpec((n, tm), lambda i: (0, i)))
    outs = pl.pallas_call(
        functools.partial(_rmsnorm_kernel, n_gains=n_gains, has_proj=has_proj),
        grid=(t // tm,),
        in_specs=in_specs,
        out_specs=out_specs,
        out_shape=out_shape,
        compiler_params=_params("parallel"),
        name="rmsnorm",
    )(*args)
    return outs


def _mm_plain_kernel(a_ref, w_ref, o_ref):
    o_ref[...] = _dot(a_ref[...], w_ref[...].astype(BF16)).astype(o_ref.dtype)


def _mm_res_kernel(a_ref, w_ref, r_ref, o_ref):
    o_ref[...] = r_ref[...] + _dot(a_ref[...], w_ref[...].astype(BF16))


def _mm_swiglu_kernel(a_ref, wg_ref, wu_ref, o_ref):
    a = a_ref[...]
    g = _dot(a, wg_ref[...].astype(BF16))
    u = _dot(a, wu_ref[...].astype(BF16))
    o_ref[...] = (g * jax.nn.sigmoid(g) * u).astype(o_ref.dtype)


def _w_spec(w, k, tn, prefix, col_block_offset=0):
    lead = (None,) * len(prefix)
    return pl.BlockSpec(
        lead + (k, tn), lambda n, m: tuple(prefix) + (0, n + col_block_offset))


def _matmul(a, w, prefix, n_out, out_dtype, residual=None, tm=1024, tn=1024, name="mm"):
    m_dim, k = a.shape
    tm, tn = min(tm, m_dim), min(tn, n_out)
    grid = (n_out // tn, m_dim // tm)
    in_specs = [pl.BlockSpec((tm, k), lambda n, m: (m, 0)), _w_spec(w, k, tn, prefix)]
    args = [a, w]
    kernel = _mm_plain_kernel
    if residual is not None:
        in_specs.append(pl.BlockSpec((tm, tn), lambda n, m: (m, n)))
        args.append(residual)
        kernel = _mm_res_kernel
    return pl.pallas_call(
        kernel,
        grid=grid,
        in_specs=in_specs,
        out_specs=pl.BlockSpec((tm, tn), lambda n, m: (m, n)),
        out_shape=jax.ShapeDtypeStruct((m_dim, n_out), out_dtype),
        compiler_params=_params("parallel", "parallel"),
        name=name,
    )(*args)


def _matmul_swiglu(a, w_gu, prefix, tm=1024, tn=512, name="mm_swiglu"):
    m_dim, k = a.shape
    f = w_gu.shape[-1] // 2
    tm, tn = min(tm, m_dim), min(tn, f)
    grid = (f // tn, m_dim // tm)
    return pl.pallas_call(
        _mm_swiglu_kernel,
        grid=grid,
        in_specs=[
            pl.BlockSpec((tm, k), lambda n, m: (m, 0)),
            _w_spec(w_gu, k, tn, prefix),
            _w_spec(w_gu, k, tn, prefix, col_block_offset=f // tn),
        ],
        out_specs=pl.BlockSpec((tm, tn), lambda n, m: (m, n)),
        out_shape=jax.ShapeDtypeStruct((m_dim, f), BF16),
        compiler_params=_params("parallel", "parallel"),
        name=name,
    )(a, w_gu, w_gu)


def _mm_down_kernel(a_ref, w_ref, r_ref, o_ref):
    @pl.when(pl.program_id(1) == 0)
    def _():
        o_ref[...] = r_ref[...]

    o_ref[...] += _dot(a_ref[...], w_ref[...].astype(BF16))


def _matmul_down(a, w, prefix, residual, tm=1024, tk=512, name="mm_down"):
    m_dim, k_dim = a.shape
    n = w.shape[-1]
    tm, tk = min(tm, m_dim), min(tk, k_dim)
    nk = k_dim // tk
    lead = (None,) * len(prefix)
    return pl.pallas_call(
        _mm_down_kernel,
        grid=(m_dim // tm, nk),
        in_specs=[
            pl.BlockSpec((tm, tk), lambda m, k: (m, k)),
            pl.BlockSpec(lead + (tk, n), lambda m, k: tuple(prefix) + (k, 0)),
            pl.BlockSpec((tm, n), lambda m, k: (m, 0)),
        ],
        out_specs=pl.BlockSpec((tm, n), lambda m, k: (m, 0)),
        out_shape=jax.ShapeDtypeStruct((m_dim, n), F32),
        compiler_params=_params("parallel", "arbitrary"),
        name=name,
    )(a, w, residual)


def _strict_lower_ones(t):
    j = lax.broadcasted_iota(jnp.int32, (t, t), 0)
    s = lax.broadcasted_iota(jnp.int32, (t, t), 1)
    return jnp.where(j > s, 1.0, 0.0).astype(BF16)


def _sb_attn_kernel(q_ref, k_ref, v_ref, o_ref, acc_ref, run_ref, ls_ref, hi_ref, lo_ref,
                    rs_ref, *, tq, tk, scale):
    i = pl.program_id(2)
    r = tq // tk
    tri = _strict_lower_ones(tk)
    row = lax.broadcasted_iota(jnp.int32, (tq, tk), 0)
    col = lax.broadcasted_iota(jnp.int32, (tq, tk), 1)

    def score_stage(j, mask, slot):
        start = pl.multiple_of(j * tk, tk)
        z = _dot_nt(q_ref[...], k_ref[pl.ds(start, tk), :]) * scale
        ls = _log_sigmoid(z)
        l1m = ls - z
        if mask is not None:
            l1m = jnp.where(mask, l1m, 0.0)
            ls = jnp.where(mask, ls, -jnp.inf)
        hi = l1m.astype(BF16)
        ls_ref[slot] = ls
        hi_ref[slot] = hi
        lo_ref[slot] = (l1m - hi.astype(F32)).astype(BF16)
        rs_ref[slot] = jnp.sum(l1m, axis=-1, keepdims=True)

    def value_stage(j, slot):
        start = pl.multiple_of(j * tk, tk)
        suffix = _dot(hi_ref[slot], tri) + _dot(lo_ref[slot], tri)
        w = jnp.exp(ls_ref[slot] + suffix + run_ref[...])
        acc_ref[...] += _dot(w.astype(BF16), v_ref[pl.ds(start, tk), :])
        run_ref[...] += rs_ref[slot]

    acc_ref[...] = jnp.zeros_like(acc_ref)
    run_ref[...] = jnp.zeros_like(run_ref)
    prev = None
    for d in reversed(range(r)):
        score_stage(i * r + d, col + d * tk < row, d)
        if prev is not None:
            value_stage(*prev)
        prev = (i * r + d, d)

    def body(t, carry):
        base = (i - 1 - t) * r
        pending = (base + r, 0)
        for d in reversed(range(r)):
            score_stage(base + d, None, d)
            value_stage(*pending)
            pending = (base + d, d)
        return carry

    lax.fori_loop(0, i, body, 0)
    value_stage(0, 0)
    o_ref[...] = acc_ref[...].astype(o_ref.dtype)


def _sb_attention(qkv, batch, seq, n_heads, head_dim, tq=ATTN_TQ, tk=ATTN_TK):
    tq, tk = min(tq, seq), min(tk, seq)
    nq = seq // tq
    r = tq // tk
    return pl.pallas_call(
        functools.partial(_sb_attn_kernel, tq=tq, tk=tk, scale=head_dim ** -0.5),
        grid=(batch, n_heads, nq),
        in_specs=[
            pl.BlockSpec((tq, head_dim), lambda b, h, i: (b * nq + i, h)),
            pl.BlockSpec((seq, head_dim), lambda b, h, i: (b, n_heads + h)),
            pl.BlockSpec((seq, head_dim), lambda b, h, i: (b, 2 * n_heads + h)),
        ],
        out_specs=pl.BlockSpec((tq, head_dim), lambda b, h, i: (b * nq + i, h)),
        out_shape=jax.ShapeDtypeStruct((batch * seq, n_heads * head_dim), BF16),
        scratch_shapes=[
            pltpu.VMEM((tq, head_dim), F32),
            pltpu.VMEM((tq, 1), F32),
            pltpu.VMEM((r, tq, tk), F32),
            pltpu.VMEM((r, tq, tk), BF16),
            pltpu.VMEM((r, tq, tk), BF16),
            pltpu.VMEM((r, tq, 1), F32),
        ],
        compiler_params=_params("parallel", "parallel", "parallel"),
        name="sb_attention",
    )(qkv, qkv, qkv)


def _fgate_cumsum_kernel(fl_ref, bf_ref, c_ref, *, seq):
    x = _log_sigmoid(fl_ref[...] + bf_ref[...])
    j = lax.broadcasted_iota(jnp.int32, (seq, seq), 0)
    t = lax.broadcasted_iota(jnp.int32, (seq, seq), 1)
    upper = jnp.where(j <= t, 1.0, 0.0).astype(BF16)
    hi, mid, lo = _split3(x)
    c_ref[...] = _dot(hi, upper) + (_dot(mid, upper) + _dot(lo, upper))


def _fgate_cumsum(flogit_t, b_f, batch, seq):
    n_heads = flogit_t.shape[0]
    return pl.pallas_call(
        functools.partial(_fgate_cumsum_kernel, seq=seq),
        grid=(batch,),
        in_specs=[
            pl.BlockSpec((n_heads, seq), lambda b: (0, b)),
            pl.BlockSpec((n_heads, 1), lambda b: (0, 0)),
        ],
        out_specs=pl.BlockSpec((None, n_heads, seq), lambda b: (b, 0, 0)),
        out_shape=jax.ShapeDtypeStruct((batch, n_heads, seq), F32),
        compiler_params=_params("parallel"),
        name="fgate_cumsum",
    )(flogit_t, b_f.reshape(n_heads, 1).astype(F32))


def _fox_attn_kernel(q_ref, k_ref, v_ref, cq_ref, ck_ref, o_ref, acc_ref, bias_ref, red_ref,
                     p_ref, *, tq, tk, scale):
    i = pl.program_id(2)
    r = tq // tk
    row = lax.broadcasted_iota(jnp.int32, (tq, tk), 0)
    col = lax.broadcasted_iota(jnp.int32, (tq, tk), 1)
    band_masks = [col + d * tk <= row for d in range(r)]

    def logits(j, mask):
        start = pl.multiple_of(j * tk, tk)
        s = (_dot_nt(q_ref[...], k_ref[pl.ds(start, tk), :]) * scale
             + bias_ref[...] - ck_ref[:, pl.ds(start, tk)])
        if mask is not None:
            s = jnp.where(mask, s, -jnp.inf)
        return s

    bias_ref[...] = jnp.broadcast_to(cq_ref[...], (tq, tk))
    red_ref[...] = jnp.full((tq, tk), -jnp.inf, F32)

    def max_step(j, mask):
        red_ref[...] = jnp.maximum(red_ref[...], logits(j, mask))

    for d in range(r):
        max_step(i * r + d, band_masks[d])

    def max_body(t, carry):
        for d in range(r):
            max_step(t * r + d, None)
        return carry

    lax.fori_loop(0, i, max_body, 0)
    row_max = jnp.max(red_ref[...], axis=-1, keepdims=True)

    bias_ref[...] = jnp.broadcast_to(cq_ref[...] - row_max, (tq, tk))
    red_ref[...] = jnp.zeros((tq, tk), F32)
    acc_ref[...] = jnp.zeros_like(acc_ref)

    def prob_stage(j, mask, slot):
        p = jnp.exp(logits(j, mask))
        red_ref[...] += p
        p_ref[slot] = p.astype(BF16)

    def value_stage(j, slot):
        start = pl.multiple_of(j * tk, tk)
        acc_ref[...] += _dot(p_ref[slot], v_ref[pl.ds(start, tk), :])

    prev = None
    for d in range(r):
        prob_stage(i * r + d, band_masks[d], d)
        if prev is not None:
            value_stage(*prev)
        prev = (i * r + d, d)

    def pv_body(t, carry):
        base = t * r
        pending = (jnp.where(t == 0, (i + 1) * r - 1, base - 1), r - 1)
        for d in range(r):
            prob_stage(base + d, None, d)
            value_stage(*pending)
            pending = (base + d, d)
        return carry

    lax.fori_loop(0, i, pv_body, 0)
    value_stage(jnp.where(i == 0, r - 1, i * r - 1), r - 1)
    denom = jnp.sum(red_ref[...], axis=-1, keepdims=True)
    o_ref[...] = (acc_ref[...] / denom).astype(o_ref.dtype)


def _fox_attention(q, kv, cum_log_f, batch, seq, n_heads, head_dim, tq=ATTN_TQ, tk=ATTN_TK):
    tq, tk = min(tq, seq), min(tk, seq)
    nq = seq // tq
    r = tq // tk
    cq = cum_log_f.reshape(batch, n_heads, seq, 1)
    ck = cum_log_f.reshape(batch, n_heads, 1, seq)
    return pl.pallas_call(
        functools.partial(_fox_attn_kernel, tq=tq, tk=tk, scale=head_dim ** -0.5),
        grid=(batch, n_heads, nq),
        in_specs=[
            pl.BlockSpec((tq, head_dim), lambda b, h, i: (b * nq + i, h)),
            pl.BlockSpec((seq, head_dim), lambda b, h, i: (b, h)),
            pl.BlockSpec((seq, head_dim), lambda b, h, i: (b, n_heads + h)),
            pl.BlockSpec((None, None, tq, 1), lambda b, h, i: (b, h, i, 0)),
            pl.BlockSpec((None, None, 1, seq), lambda b, h, i: (b, h, 0, 0)),
        ],
        out_specs=pl.BlockSpec((tq, head_dim), lambda b, h, i: (b * nq + i, h)),
        out_shape=jax.ShapeDtypeStruct((batch * seq, n_heads * head_dim), BF16),
        scratch_shapes=[
            pltpu.VMEM((tq, head_dim), F32),
            pltpu.VMEM((tq, tk), F32),
            pltpu.VMEM((tq, tk), F32),
            pltpu.VMEM((r, tq, tk), BF16),
        ],
        compiler_params=_params("parallel", "parallel", "parallel"),
        name="fox_attention",
    )(q, kv, kv, cq, ck)


def _route_kernel(l_ref, w_ref, pos_ref, meta_ref, *, tm, chunk):
    l = l_ref[...]
    n_exp, t = l.shape
    e_idx = lax.broadcasted_iota(jnp.int32, l.shape, 0)
    m1 = jnp.max(l, axis=0, keepdims=True)
    i1 = jnp.min(jnp.where(l == m1, e_idx, n_exp), axis=0, keepdims=True)
    sel1 = e_idx == i1
    l2 = jnp.where(sel1, -jnp.inf, l)
    m2 = jnp.max(l2, axis=0, keepdims=True)
    i2 = jnp.min(jnp.where(l2 == m2, e_idx, n_exp), axis=0, keepdims=True)
    sel2 = e_idx == i2
    e2 = jnp.exp(m2 - m1)
    denom = 1.0 + e2
    w_ref[0:1, :] = 1.0 / denom
    w_ref[1:2, :] = e2 / denom

    s1 = jnp.where(sel1, 1.0, 0.0)
    s2 = jnp.where(sel2, 1.0, 0.0)
    s_any = (s1 + s2).astype(BF16)
    cnt_col = jnp.sum(s1 + s2, axis=1, keepdims=True)
    cnt_row = _dot_nt(jnp.ones((n_exp, t), BF16), s_any)

    def padded(c):
        return jnp.floor((c + (tm - 1)) * (1.0 / tm)) * tm

    ei = lax.broadcasted_iota(jnp.int32, (n_exp, n_exp), 0)
    ej = lax.broadcasted_iota(jnp.int32, (n_exp, n_exp), 1)
    start_col = jnp.sum(jnp.where(ej < ei, padded(cnt_row), 0.0), axis=1, keepdims=True)
    end_col = start_col + padded(cnt_col)

    jj = lax.broadcasted_iota(jnp.int32, (chunk, chunk), 0)
    tt = lax.broadcasted_iota(jnp.int32, (chunk, chunk), 1)
    before = jnp.where(jj < tt, 1.0, 0.0).astype(BF16)
    carry = jnp.zeros((n_exp, 1), F32)
    for c in range(t // chunk):
        sl = slice(c * chunk, (c + 1) * chunk)
        dest = start_col + carry + _dot(s_any[:, sl], before)
        pos_ref[0:1, sl] = jnp.sum(s1[:, sl] * dest, axis=0, keepdims=True).astype(jnp.int32)
        pos_ref[1:2, sl] = jnp.sum(s2[:, sl] * dest, axis=0, keepdims=True).astype(jnp.int32)
        carry = carry + jnp.sum(s1[:, sl] + s2[:, sl], axis=1, keepdims=True)

    lanes = meta_ref.shape[1]
    blk_start = lax.broadcasted_iota(jnp.int32, (n_exp, lanes), 1).astype(F32) * tm
    blk_expert = jnp.sum(jnp.where(end_col <= blk_start, 1.0, 0.0), axis=0, keepdims=True)
    blk_expert = jnp.minimum(blk_expert, n_exp - 1.0)
    n_used = jnp.max(end_col, axis=0, keepdims=True) * (1.0 / tm)
    in_group = (start_col <= blk_start) & (blk_start < end_col)
    blk_rows = jnp.sum(
        jnp.where(in_group, jnp.clip(start_col + cnt_col - blk_start, 0.0, tm), 0.0),
        axis=0, keepdims=True)
    meta_row = lax.broadcasted_iota(jnp.int32, meta_ref.shape, 0)
    meta = jnp.where(meta_row == 0, blk_expert, jnp.where(meta_row == 1, n_used, blk_rows))
    meta_ref[...] = meta.astype(jnp.int32)


def _route(logits_t, tm):
    n_exp, t = logits_t.shape
    assert tm & (tm - 1) == 0, "row tile must be a power of two"
    assert TOP_K * t // tm + n_exp <= VREG_LANES
    chunk = min(1024, t)
    return pl.pallas_call(
        functools.partial(_route_kernel, tm=tm, chunk=chunk),
        out_shape=[
            jax.ShapeDtypeStruct((TOP_K, t), F32),
            jax.ShapeDtypeStruct((TOP_K, t), jnp.int32),
            jax.ShapeDtypeStruct((VREG_SUBLANES, VREG_LANES), jnp.int32),
        ],
        compiler_params=pltpu.CompilerParams(vmem_limit_bytes=V7X_VMEM_LIMIT_BYTES),
        name="moe_route",
    )(logits_t)


def _dispatch_kernel(pos_ref, x_ref, init_hbm, o_hbm, sem, *, tb):
    del init_hbm

    def issue(r, carry):
        for c in range(TOP_K):
            pltpu.make_async_copy(x_ref.at[r], o_hbm.at[pos_ref[c, r]], sem).start()
        return carry

    lax.fori_loop(0, tb, issue, 0)
    for c in range(TOP_K):
        pltpu.make_async_copy(x_ref, o_hbm.at[pl.ds(0, tb)], sem).wait()


def _dispatch(pos, x, m_pad, tb=256):
    t, d = x.shape
    tb = min(tb, t)
    x3 = x.reshape(t, d // VREG_LANES, VREG_LANES)
    out = pl.pallas_call(
        functools.partial(_dispatch_kernel, tb=tb),
        grid=(t // tb,),
        in_specs=[
            pl.BlockSpec((TOP_K, tb), lambda i: (0, i), memory_space=pltpu.SMEM),
            pl.BlockSpec((tb,) + x3.shape[1:], lambda i: (i, 0, 0)),
            pl.BlockSpec(memory_space=pl.ANY),
        ],
        out_specs=pl.BlockSpec(memory_space=pl.ANY),
        out_shape=jax.ShapeDtypeStruct((m_pad,) + x3.shape[1:], x.dtype),
        scratch_shapes=[pltpu.SemaphoreType.DMA(())],
        input_output_aliases={2: 0},
        compiler_params=_params("arbitrary"),
        name="moe_dispatch",
    )(pos, x3, jnp.zeros((m_pad,) + x3.shape[1:], x.dtype))
    return out.reshape(m_pad, d)


META_EXPERT, META_N_USED, META_ROWS = 0, 1, 2


def _clamp_block(b, meta_ref):
    return jnp.minimum(b, meta_ref[META_N_USED, 0] - 1)


def _block_expert(b, meta_ref):
    return meta_ref[META_EXPERT, _clamp_block(b, meta_ref)]


def _gmm_swiglu_kernel(meta_ref, a_ref, wg_ref, wu_ref, wd_ref, o_ref, wd_bf16_ref, *, tm):
    rows = meta_ref[META_ROWS, pl.program_id(1)]
    half = tm // 2

    def branch(n_rows):
        wd_bf16_ref[...] = wd_ref[...].astype(BF16)
        if n_rows:
            a = a_ref[:n_rows]
            g = _dot(a, wg_ref[...].astype(BF16))
            u = _dot(a, wu_ref[...].astype(BF16))
            o_ref[:n_rows] = (g * jax.nn.sigmoid(g) * u).astype(o_ref.dtype)
        if n_rows < tm:
            o_ref[n_rows:] = jnp.zeros((tm - n_rows,) + o_ref.shape[1:], o_ref.dtype)

    pl.when(rows > half)(lambda: branch(tm))
    pl.when((rows > 0) & (rows <= half))(lambda: branch(half))
    pl.when(rows == 0)(lambda: branch(0))


def _cast_chunk_rows(total_rows, n_steps):
    bf16_tile_rows = 2 * VREG_SUBLANES
    for rows in range(bf16_tile_rows, total_rows + 1, bf16_tile_rows):
        if total_rows % rows == 0 and total_rows // rows <= n_steps:
            return rows
    raise ValueError("no chunking of the down weights fits the grid")


def _grouped_swiglu(meta, a, w_gu, w_down, layer, tm, tn=1024):
    m_pad, k = a.shape
    f = w_gu.shape[-1] // 2
    tn = min(tn, f)
    nb = m_pad // tm
    rows, d = w_down.shape
    chunk = _cast_chunk_rows(rows, (f // tn) * nb)
    last_chunk = rows // chunk - 1

    def a_map(n, b, meta):
        return (_clamp_block(b, meta), 0)

    def w_map(off):
        return lambda n, b, meta: (layer, _block_expert(b, meta), 0, n + off)

    def chunk_map(n, b, meta):
        return (jnp.minimum(n * nb + b, last_chunk), 0)

    return pl.pallas_call(
        functools.partial(_gmm_swiglu_kernel, tm=tm),
        grid_spec=pltpu.PrefetchScalarGridSpec(
            num_scalar_prefetch=1,
            grid=(f // tn, nb),
            in_specs=[
                pl.BlockSpec((tm, k), a_map),
                pl.BlockSpec((None, None, k, tn), w_map(0)),
                pl.BlockSpec((None, None, k, tn), w_map(f // tn)),
                pl.BlockSpec((chunk, d), chunk_map),
            ],
            out_specs=[
                pl.BlockSpec((tm, tn), lambda n, b, meta: (b, n)),
                pl.BlockSpec((chunk, d), chunk_map),
            ],
        ),
        out_shape=[
            jax.ShapeDtypeStruct((m_pad, f), BF16),
            jax.ShapeDtypeStruct((rows, d), BF16),
        ],
        compiler_params=_params("arbitrary", "arbitrary"),
        name="moe_gu",
    )(meta, a, w_gu, w_gu, w_down)


def _gmm_down_kernel(meta_ref, a_ref, w_ref, o_ref, *, tm):
    rows = meta_ref[META_ROWS, pl.program_id(0)]
    half = tm // 2

    @pl.when(pl.program_id(1) == 0)
    def _():
        o_ref[...] = jnp.zeros_like(o_ref)

    @pl.when(rows > half)
    def _():
        o_ref[...] += _dot(a_ref[...], w_ref[...])

    @pl.when((rows > 0) & (rows <= half))
    def _():
        o_ref[:half] += _dot(a_ref[:half], w_ref[...])


def _grouped_down(meta, a, w_down_bf16, tm, tk=1024):
    m_pad, k_dim = a.shape
    n = w_down_bf16.shape[-1]
    tk = min(tk, k_dim)
    nk = k_dim // tk
    nb = m_pad // tm

    def k_eff(b, k, meta):
        return jnp.where(b < meta[META_N_USED, 0], k, nk - 1)

    return pl.pallas_call(
        functools.partial(_gmm_down_kernel, tm=tm),
        grid_spec=pltpu.PrefetchScalarGridSpec(
            num_scalar_prefetch=1,
            grid=(nb, nk),
            in_specs=[
                pl.BlockSpec((tm, tk),
                             lambda b, k, meta: (_clamp_block(b, meta), k_eff(b, k, meta))),
                pl.BlockSpec((tk, n),
                             lambda b, k, meta: (_block_expert(b, meta) * nk + k_eff(b, k, meta),
                                                 0)),
            ],
            out_specs=pl.BlockSpec((tm, n), lambda b, k, meta: (b, 0)),
        ),
        out_shape=jax.ShapeDtypeStruct((m_pad, n), F32),
        compiler_params=_params("arbitrary", "arbitrary"),
        name="moe_down",
    )(meta, a, w_down_bf16)


def _combine_kernel(pos_ref, pos_next_ref, y_hbm, h_ref, w_ref, g_ref, o_ref, buf_ref, sems,
                    *, tb, n_steps):
    i = pl.program_id(0)
    slot = lax.rem(i, 2)

    def gather(p_ref, s):
        def issue(r, carry):
            for c in range(TOP_K):
                pltpu.make_async_copy(y_hbm.at[pl.ds(p_ref[c, r], 1)],
                                      buf_ref.at[s, c, pl.ds(r, 1)], sems.at[s]).start()
            return carry

        lax.fori_loop(0, tb, issue, 0)

    @pl.when(i == 0)
    def _():
        gather(pos_ref, 0)

    @pl.when(i + 1 < n_steps)
    def _():
        gather(pos_next_ref, 1 - slot)

    for c in range(TOP_K):
        pltpu.make_async_copy(y_hbm.at[pl.ds(0, tb)], buf_ref.at[slot, c], sems.at[slot]).wait()
    w = w_ref[...]
    h = h_ref[...] + (w[:, 0:1] * buf_ref[slot, 0] + w[:, 1:2] * buf_ref[slot, 1])
    y = h * lax.rsqrt(jnp.mean(h * h, axis=-1, keepdims=True) + RMS_EPS)
    o_ref[...] = (y * g_ref[...]).astype(o_ref.dtype)


def _combine_norm(pos, y_sorted, h, gate_w, gain, out_dtype, tb=256):
    t, d = h.shape
    tb = min(tb, t)
    n_steps = t // tb
    return pl.pallas_call(
        functools.partial(_combine_kernel, tb=tb, n_steps=n_steps),
        grid=(n_steps,),
        in_specs=[
            pl.BlockSpec((TOP_K, tb), lambda i: (0, i), memory_space=pltpu.SMEM),
            pl.BlockSpec((TOP_K, tb), lambda i: (0, jnp.minimum(i + 1, n_steps - 1)),
                         memory_space=pltpu.SMEM),
            pl.BlockSpec(memory_space=pl.ANY),
            pl.BlockSpec((tb, d), lambda i: (i, 0)),
            pl.BlockSpec((tb, TOP_K), lambda i: (i, 0)),
            pl.BlockSpec((1, d), lambda i: (0, 0)),
        ],
        out_specs=pl.BlockSpec((tb, d), lambda i: (i, 0)),
        out_shape=jax.ShapeDtypeStruct((t, d), out_dtype),
        scratch_shapes=[pltpu.VMEM((2, TOP_K, tb, d), F32), pltpu.SemaphoreType.DMA((2,))],
        compiler_params=_params("arbitrary"),
        name="moe_combine_norm",
    )(pos, pos, y_sorted, h, gate_w, gain.reshape(1, d).astype(F32))


def kernel(x, norm_mix, norm_ffn, w_qkv_a, w_o_a, norm_kv, w_kvf_b, b_f, w_q_b, w_o_b,
           w_gu_dense, w_down_dense, w_router, w_gu_exp, w_down_exp, norm_final):
    batch, seq, d_model = x.shape
    n_heads = b_f.shape[0]
    d_attn = w_o_a.shape[1]
    head_dim = d_attn // n_heads
    n_experts = w_router.shape[-1]
    assert TOP_K == 2 and norm_mix.shape[0] == 2, "kernel is written for the depth-2 trunk"
    t = batch * seq
    h = x.reshape(t, d_model)

    (hn,) = _rmsnorm(h, [norm_mix[0]], BF16)
    qkv = _matmul(hn, w_qkv_a, (0,), 3 * d_attn, BF16, name="mm_qkv_a")
    o = _sb_attention(qkv, batch, seq, n_heads, head_dim)
    h = _matmul(o, w_o_a, (0,), d_model, F32, residual=h, name="mm_o_a")
    (hn,) = _rmsnorm(h, [norm_ffn[0]], BF16)
    act = _matmul_swiglu(hn, w_gu_dense, (0,), name="mm_gu_dense")
    h = _matmul_down(act, w_down_dense, (0,), h, name="mm_down_dense")

    w_f_t = w_kvf_b[:, 2 * d_attn:].T
    hn_q, hn_kv, flogit_t = _rmsnorm(h, [norm_mix[1], norm_kv], BF16, proj_t=w_f_t)
    kv = _matmul(hn_kv, w_kvf_b, (), 2 * d_attn, BF16, name="mm_kv_b")
    q = _matmul(hn_q, w_q_b, (0,), d_attn, BF16, name="mm_q_b")
    cum_log_f = _fgate_cumsum(flogit_t, b_f, batch, seq)
    o = _fox_attention(q, kv, cum_log_f, batch, seq, n_heads, head_dim)
    h = _matmul(o, w_o_b, (0,), d_model, F32, residual=h, name="mm_o_b")

    hn, logits_t = _rmsnorm(h, [norm_ffn[1]], BF16, proj_t=w_router[0].T)
    tm = min(MOE_ROW_TILE, t)
    m_pad = TOP_K * t + n_experts * tm
    gate_w, pos, meta = _route(logits_t, tm)
    x_sorted = _dispatch(pos, hn, m_pad)
    w_down = w_down_exp[0].reshape(-1, d_model)
    act, w_down_bf16 = _grouped_swiglu(meta, x_sorted, w_gu_exp, w_down, 0, tm)
    y_sorted = _grouped_down(meta, act, w_down_bf16, tm)
    out = _combine_norm(pos, y_sorted, h, gate_w.T, norm_final, x.dtype)
    return out.reshape(batch, seq, d_model)
```

```python
import functools

import jax
import jax.numpy as jnp
from jax import lax
from jax.experimental import pallas as pl
from jax.experimental.pallas import tpu as pltpu

RMS_EPS = 1e-6
TOP_K = 2

V7X_VMEM_LIMIT_BYTES = 56 * 1024 * 1024

VREG_SUBLANES = 8
VREG_LANES = 128

MOE_ROW_TILE = 1024
MOE_SUB_ROWS = 256
ATTN_TQ = 512
ATTN_TK = 256

BF16 = jnp.bfloat16
F32 = jnp.float32


def _params(*semantics):
    return pltpu.CompilerParams(
        dimension_semantics=semantics, vmem_limit_bytes=V7X_VMEM_LIMIT_BYTES)


def _dot(a, b):
    return jnp.dot(a, b, preferred_element_type=F32)


def _dot_nt(a, b):
    return lax.dot_general(a, b, (((1,), (1,)), ((), ())), preferred_element_type=F32)


def _split3(x):
    hi = x.astype(BF16)
    r1 = x - hi.astype(F32)
    mid = r1.astype(BF16)
    lo = (r1 - mid.astype(F32)).astype(BF16)
    return hi, mid, lo


def _log_sigmoid(z):
    return jnp.minimum(z, 0.0) - jnp.log(1.0 + jnp.exp(-jnp.abs(z)))


def _rmsnorm_kernel(*refs, n_gains, has_proj):
    x_ref = refs[0]
    g_refs = refs[1:1 + n_gains]
    pos = 1 + n_gains
    p_ref = refs[pos] if has_proj else None
    pos += int(has_proj)
    o_refs = refs[pos:pos + n_gains]
    po_ref = refs[pos + n_gains] if has_proj else None

    x = x_ref[...]
    y = x * lax.rsqrt(jnp.mean(x * x, axis=-1, keepdims=True) + RMS_EPS)
    for g_ref, o_ref in zip(g_refs, o_refs):
        o_ref[...] = (y * g_ref[...]).astype(o_ref.dtype)
    if has_proj:
        yn = y * g_refs[-1][...]
        yh, ym, _ = _split3(yn)
        ph, pm, _ = _split3(p_ref[...])
        po_ref[...] = _dot_nt(ph, yh) + (_dot_nt(ph, ym) + _dot_nt(pm, yh))


def _rmsnorm(x, gains, out_dtype, proj_t=None, tm=256):
    t, d = x.shape
    tm = min(tm, t)
    n_gains = len(gains)
    has_proj = proj_t is not None
    in_specs = [pl.BlockSpec((tm, d), lambda i: (i, 0))]
    in_specs += [pl.BlockSpec((1, d), lambda i: (0, 0))] * n_gains
    args = [x] + [g.reshape(1, d).astype(F32) for g in gains]
    out_shape = [jax.ShapeDtypeStruct((t, d), out_dtype)] * n_gains
    out_specs = [pl.BlockSpec((tm, d), lambda i: (i, 0))] * n_gains
    if has_proj:
        n = proj_t.shape[0]
        in_specs.append(pl.BlockSpec((n, d), lambda i: (0, 0)))
        args.append(proj_t)
        out_shape.append(jax.ShapeDtypeStruct((n, t), F32))
        out_specs.append(pl.BlockSpec((n, tm), lambda i: (0, i)))
    outs = pl.pallas_call(
        functools.partial(_rmsnorm_kernel, n_gains=n_gains, has_proj=has_proj),
        grid=(t // tm,),
        in_specs=in_specs,
        out_specs=out_specs,
        out_shape=out_shape,
        compiler_params=_params("parallel"),
        name="rmsnorm",
    )(*args)
    return outs


def _mm_plain_kernel(a_ref, w_ref, o_ref):
    o_ref[...] = _dot(a_ref[...], w_ref[...].astype(BF16)).astype(o_ref.dtype)


def _mm_res_kernel(a_ref, w_ref, r_ref, o_ref):
    o_ref[...] = r_ref[...] + _dot(a_ref[...], w_ref[...].astype(BF16))


def _mm_swiglu_kernel(a_ref, wg_ref, wu_ref, o_ref):
    a = a_ref[...]
    g = _dot(a, wg_ref[...].astype(BF16))
    u = _dot(a, wu_ref[...].astype(BF16))
    o_ref[...] = (g * jax.nn.sigmoid(g) * u).astype(o_ref.dtype)


def _w_spec(w, k, tn, prefix, col_block_offset=0):
    lead = (None,) * len(prefix)
    return pl.BlockSpec(
        lead + (k, tn), lambda n, m: tuple(prefix) + (0, n + col_block_offset))


def _matmul(a, w, prefix, n_out, out_dtype, residual=None, tm=1024, tn=1024, name="mm"):
    m_dim, k = a.shape
    tm, tn = min(tm, m_dim), min(tn, n_out)
    grid = (n_out // tn, m_dim // tm)
    in_specs = [pl.BlockSpec((tm, k), lambda n, m: (m, 0)), _w_spec(w, k, tn, prefix)]
    args = [a, w]
    kernel = _mm_plain_kernel
    if residual is not None:
        in_specs.append(pl.BlockSpec((tm, tn), lambda n, m: (m, n)))
        args.append(residual)
        kernel = _mm_res_kernel
    return pl.pallas_call(
        kernel,
        grid=grid,
        in_specs=in_specs,
        out_specs=pl.BlockSpec((tm, tn), lambda n, m: (m, n)),
        out_shape=jax.ShapeDtypeStruct((m_dim, n_out), out_dtype),
        compiler_params=_params("parallel", "parallel"),
        name=name,
    )(*args)


def _matmul_swiglu(a, w_gu, prefix, tm=1024, tn=512, name="mm_swiglu"):
    m_dim, k = a.shape
    f = w_gu.shape[-1] // 2
    tm, tn = min(tm, m_dim), min(tn, f)
    grid = (f // tn, m_dim // tm)
    return pl.pallas_call(
        _mm_swiglu_kernel,
        grid=grid,
        in_specs=[
            pl.BlockSpec((tm, k), lambda n, m: (m, 0)),
            _w_spec(w_gu, k, tn, prefix),
            _w_spec(w_gu, k, tn, prefix, col_block_offset=f // tn),
        ],
        out_specs=pl.BlockSpec((tm, tn), lambda n, m: (m, n)),
        out_shape=jax.ShapeDtypeStruct((m_dim, f), BF16),
        compiler_params=_params("parallel", "parallel"),
        name=name,
    )(a, w_gu, w_gu)


def _mm_down_kernel(a_ref, w_ref, r_ref, o_ref):
    @pl.when(pl.program_id(1) == 0)
    def _():
        o_ref[...] = r_ref[...]

    o_ref[...] += _dot(a_ref[...], w_ref[...].astype(BF16))


def _matmul_down(a, w, prefix, residual, tm=1024, tk=512, name="mm_down"):
    m_dim, k_dim = a.shape
    n = w.shape[-1]
    tm, tk = min(tm, m_dim), min(tk, k_dim)
    nk = k_dim // tk
    lead = (None,) * len(prefix)
    return pl.pallas_call(
        _mm_down_kernel,
        grid=(m_dim // tm, nk),
        in_specs=[
            pl.BlockSpec((tm, tk), lambda m, k: (m, k)),
            pl.BlockSpec(lead + (tk, n), lambda m, k: tuple(prefix) + (k, 0)),
            pl.BlockSpec((tm, n), lambda m, k: (m, 0)),
        ],
        out_specs=pl.BlockSpec((tm, n), lambda m, k: (m, 0)),
        out_shape=jax.ShapeDtypeStruct((m_dim, n), F32),
        compiler_params=_params("parallel", "arbitrary"),
        name=name,
    )(a, w, residual)


def _strict_lower_ones(t):
    j = lax.broadcasted_iota(jnp.int32, (t, t), 0)
    s = lax.broadcasted_iota(jnp.int32, (t, t), 1)
    return jnp.where(j > s, 1.0, 0.0).astype(BF16)


def _sb_attn_kernel(q_ref, k_ref, v_ref, o_ref, acc_ref, run_ref, ls_ref, hi_ref, lo_ref,
                    rs_ref, *, tq, tk, scale):
    i = pl.program_id(2)
    r = tq // tk
    tri = _strict_lower_ones(tk)
    row = lax.broadcasted_iota(jnp.int32, (tq, tk), 0)
    col = lax.broadcasted_iota(jnp.int32, (tq, tk), 1)

    def score_stage(j, mask, slot):
        start = pl.multiple_of(j * tk, tk)
        z = _dot_nt(q_ref[...], k_ref[pl.ds(start, tk), :]) * scale
        ls = _log_sigmoid(z)
        l1m = ls - z
        if mask is not None:
            l1m = jnp.where(mask, l1m, 0.0)
            ls = jnp.where(mask, ls, -jnp.inf)
        hi = l1m.astype(BF16)
        ls_ref[slot] = ls
        hi_ref[slot] = hi
        lo_ref[slot] = (l1m - hi.astype(F32)).astype(BF16)
        rs_ref[slot] = jnp.sum(l1m, axis=-1, keepdims=True)

    def value_stage(j, slot):
        start = pl.multiple_of(j * tk, tk)
        suffix = _dot(hi_ref[slot], tri) + _dot(lo_ref[slot], tri)
        w = jnp.exp(ls_ref[slot] + suffix + run_ref[...])
        acc_ref[...] += _dot(w.astype(BF16), v_ref[pl.ds(start, tk), :])
        run_ref[...] += rs_ref[slot]

    acc_ref[...] = jnp.zeros_like(acc_ref)
    run_ref[...] = jnp.zeros_like(run_ref)
    prev = None
    for d in reversed(range(r)):
        score_stage(i * r + d, col + d * tk < row, d)
        if prev is not None:
            value_stage(*prev)
        prev = (i * r + d, d)

    def body(t, carry):
        base = (i - 1 - t) * r
        pending = (base + r, 0)
        for d in reversed(range(r)):
            score_stage(base + d, None, d)
            value_stage(*pending)
            pending = (base + d, d)
        return carry

    lax.fori_loop(0, i, body, 0)
    value_stage(0, 0)
    o_ref[...] = acc_ref[...].astype(o_ref.dtype)


def _sb_attention(qkv, batch, seq, n_heads, head_dim, tq=ATTN_TQ, tk=ATTN_TK):
    tq, tk = min(tq, seq), min(tk, seq)
    nq = seq // tq
    r = tq // tk
    return pl.pallas_call(
        functools.partial(_sb_attn_kernel, tq=tq, tk=tk, scale=head_dim ** -0.5),
        grid=(batch, n_heads, nq),
        in_specs=[
            pl.BlockSpec((tq, head_dim), lambda b, h, i: (b * nq + i, h)),
            pl.BlockSpec((seq, head_dim), lambda b, h, i: (b, n_heads + h)),
            pl.BlockSpec((seq, head_dim), lambda b, h, i: (b, 2 * n_heads + h)),
        ],
        out_specs=pl.BlockSpec((tq, head_dim), lambda b, h, i: (b * nq + i, h)),
        out_shape=jax.ShapeDtypeStruct((batch * seq, n_heads * head_dim), BF16),
        scratch_shapes=[
            pltpu.VMEM((tq, head_dim), F32),
            pltpu.VMEM((tq, 1), F32),
            pltpu.VMEM((r, tq, tk), F32),
            pltpu.VMEM((r, tq, tk), BF16),
            pltpu.VMEM((r, tq, tk), BF16),
            pltpu.VMEM((r, tq, 1), F32),
        ],
        compiler_params=_params("parallel", "parallel", "parallel"),
        name="sb_attention",
    )(qkv, qkv, qkv)


def _fgate_cumsum_kernel(fl_ref, bf_ref, c_ref, *, seq):
    x = _log_sigmoid(fl_ref[...] + bf_ref[...])
    j = lax.broadcasted_iota(jnp.int32, (seq, seq), 0)
    t = lax.broadcasted_iota(jnp.int32, (seq, seq), 1)
    upper = jnp.where(j <= t, 1.0, 0.0).astype(BF16)
    hi, mid, lo = _split3(x)
    c_ref[...] = _dot(hi, upper) + (_dot(mid, upper) + _dot(lo, upper))


def _fgate_cumsum(flogit_t, b_f, batch, seq):
    n_heads = flogit_t.shape[0]
    return pl.pallas_call(
        functools.partial(_fgate_cumsum_kernel, seq=seq),
        grid=(batch,),
        in_specs=[
            pl.BlockSpec((n_heads, seq), lambda b: (0, b)),
            pl.BlockSpec((n_heads, 1), lambda b: (0, 0)),
        ],
        out_specs=pl.BlockSpec((None, n_heads, seq), lambda b: (b, 0, 0)),
        out_shape=jax.ShapeDtypeStruct((batch, n_heads, seq), F32),
        compiler_params=_params("parallel"),
        name="fgate_cumsum",
    )(flogit_t, b_f.reshape(n_heads, 1).astype(F32))


def _fox_attn_kernel(q_ref, k_ref, v_ref, cq_ref, ck_ref, o_ref, acc_ref, bias_ref, red_ref,
                     p_ref, *, tq, tk, scale):
    i = pl.program_id(2)
    r = tq // tk
    row = lax.broadcasted_iota(jnp.int32, (tq, tk), 0)
    col = lax.broadcasted_iota(jnp.int32, (tq, tk), 1)
    band_masks = [col + d * tk <= row for d in range(r)]

    def logits(j, mask):
        start = pl.multiple_of(j * tk, tk)
        s = (_dot_nt(q_ref[...], k_ref[pl.ds(start, tk), :]) * scale
             + bias_ref[...] - ck_ref[:, pl.ds(start, tk)])
        if mask is not None:
            s = jnp.where(mask, s, -jnp.inf)
        return s

    bias_ref[...] = jnp.broadcast_to(cq_ref[...], (tq, tk))
    red_ref[...] = jnp.full((tq, tk), -jnp.inf, F32)

    def max_step(j, mask):
        red_ref[...] = jnp.maximum(red_ref[...], logits(j, mask))

    for d in range(r):
        max_step(i * r + d, band_masks[d])

    def max_body(t, carry):
        for d in range(r):
            max_step(t * r + d, None)
        return carry

    lax.fori_loop(0, i, max_body, 0)
    row_max = jnp.max(red_ref[...], axis=-1, keepdims=True)

    bias_ref[...] = jnp.broadcast_to(cq_ref[...] - row_max, (tq, tk))
    red_ref[...] = jnp.zeros((tq, tk), F32)
    acc_ref[...] = jnp.zeros_like(acc_ref)

    def prob_stage(j, mask, slot):
        p = jnp.exp(logits(j, mask))
        red_ref[...] += p
        p_ref[slot] = p.astype(BF16)

    def value_stage(j, slot):
        start = pl.multiple_of(j * tk, tk)
        acc_ref[...] += _dot(p_ref[slot], v_ref[pl.ds(start, tk), :])

    prev = None
    for d in range(r):
        prob_stage(i * r + d, band_masks[d], d)
        if prev is not None:
            value_stage(*prev)
        prev = (i * r + d, d)

    def pv_body(t, carry):
        base = t * r
        pending = (jnp.where(t == 0, (i + 1) * r - 1, base - 1), r - 1)
        for d in range(r):
            prob_stage(base + d, None, d)
            value_stage(*pending)
            pending = (base + d, d)
        return carry

    lax.fori_loop(0, i, pv_body, 0)
    value_stage(jnp.where(i == 0, r - 1, i * r - 1), r - 1)
    denom = jnp.sum(red_ref[...], axis=-1, keepdims=True)
    o_ref[...] = (acc_ref[...] / denom).astype(o_ref.dtype)


def _fox_attention(q, kv, cum_log_f, batch, seq, n_heads, head_dim, tq=ATTN_TQ, tk=ATTN_TK):
    tq, tk = min(tq, seq), min(tk, seq)
    nq = seq // tq
    r = tq // tk
    cq = cum_log_f.reshape(batch, n_heads, seq, 1)
    ck = cum_log_f.reshape(batch, n_heads, 1, seq)
    return pl.pallas_call(
        functools.partial(_fox_attn_kernel, tq=tq, tk=tk, scale=head_dim ** -0.5),
        grid=(batch, n_heads, nq),
        in_specs=[
            pl.BlockSpec((tq, head_dim), lambda b, h, i: (b * nq + i, h)),
            pl.BlockSpec((seq, head_dim), lambda b, h, i: (b, h)),
            pl.BlockSpec((seq, head_dim), lambda b, h, i: (b, n_heads + h)),
            pl.BlockSpec((None, None, tq, 1), lambda b, h, i: (b, h, i, 0)),
            pl.BlockSpec((None, None, 1, seq), lambda b, h, i: (b, h, 0, 0)),
        ],
        out_specs=pl.BlockSpec((tq, head_dim), lambda b, h, i: (b * nq + i, h)),
        out_shape=jax.ShapeDtypeStruct((batch * seq, n_heads * head_dim), BF16),
        scratch_shapes=[
            pltpu.VMEM((tq, head_dim), F32),
            pltpu.VMEM((tq, tk), F32),
            pltpu.VMEM((tq, tk), F32),
            pltpu.VMEM((r, tq, tk), BF16),
        ],
        compiler_params=_params("parallel", "parallel", "parallel"),
        name="fox_attention",
    )(q, kv, kv, cq, ck)


def _route_kernel(l_ref, w_ref, pos_ref, meta_ref, *, tm, chunk):
    l = l_ref[...]
    n_exp, t = l.shape
    e_idx = lax.broadcasted_iota(jnp.int32, l.shape, 0)
    m1 = jnp.max(l, axis=0, keepdims=True)
    i1 = jnp.min(jnp.where(l == m1, e_idx, n_exp), axis=0, keepdims=True)
    sel1 = e_idx == i1
    l2 = jnp.where(sel1, -jnp.inf, l)
    m2 = jnp.max(l2, axis=0, keepdims=True)
    i2 = jnp.min(jnp.where(l2 == m2, e_idx, n_exp), axis=0, keepdims=True)
    sel2 = e_idx == i2
    e2 = jnp.exp(m2 - m1)
    denom = 1.0 + e2
    w_ref[0:1, :] = 1.0 / denom
    w_ref[1:2, :] = e2 / denom

    s1 = jnp.where(sel1, 1.0, 0.0)
    s2 = jnp.where(sel2, 1.0, 0.0)
    s_any = (s1 + s2).astype(BF16)
    cnt_col = jnp.sum(s1 + s2, axis=1, keepdims=True)
    cnt_row = _dot_nt(jnp.ones((n_exp, t), BF16), s_any)

    def padded(c):
        return jnp.floor((c + (tm - 1)) * (1.0 / tm)) * tm

    ei = lax.broadcasted_iota(jnp.int32, (n_exp, n_exp), 0)
    ej = lax.broadcasted_iota(jnp.int32, (n_exp, n_exp), 1)
    start_col = jnp.sum(jnp.where(ej < ei, padded(cnt_row), 0.0), axis=1, keepdims=True)
    end_col = start_col + padded(cnt_col)

    jj = lax.broadcasted_iota(jnp.int32, (chunk, chunk), 0)
    tt = lax.broadcasted_iota(jnp.int32, (chunk, chunk), 1)
    before = jnp.where(jj < tt, 1.0, 0.0).astype(BF16)
    carry = jnp.zeros((n_exp, 1), F32)
    for c in range(t // chunk):
        sl = slice(c * chunk, (c + 1) * chunk)
        dest = start_col + carry + _dot(s_any[:, sl], before)
        pos_ref[0:1, sl] = jnp.sum(s1[:, sl] * dest, axis=0, keepdims=True).astype(jnp.int32)
        pos_ref[1:2, sl] = jnp.sum(s2[:, sl] * dest, axis=0, keepdims=True).astype(jnp.int32)
        carry = carry + jnp.sum(s1[:, sl] + s2[:, sl], axis=1, keepdims=True)

    lanes = meta_ref.shape[1]
    blk_start = lax.broadcasted_iota(jnp.int32, (n_exp, lanes), 1).astype(F32) * tm
    blk_expert = jnp.sum(jnp.where(end_col <= blk_start, 1.0, 0.0), axis=0, keepdims=True)
    blk_expert = jnp.minimum(blk_expert, n_exp - 1.0)
    n_used = jnp.max(end_col, axis=0, keepdims=True) * (1.0 / tm)
    in_group = (start_col <= blk_start) & (blk_start < end_col)
    blk_rows = jnp.sum(
        jnp.where(in_group, jnp.clip(start_col + cnt_col - blk_start, 0.0, tm), 0.0),
        axis=0, keepdims=True)
    meta_row = lax.broadcasted_iota(jnp.int32, meta_ref.shape, 0)
    meta = jnp.where(meta_row == 0, blk_expert, jnp.where(meta_row == 1, n_used, blk_rows))
    meta_ref[...] = meta.astype(jnp.int32)


def _route(logits_t, tm):
    n_exp, t = logits_t.shape
    assert tm & (tm - 1) == 0, "row tile must be a power of two"
    assert TOP_K * t // tm + n_exp <= VREG_LANES
    chunk = min(1024, t)
    return pl.pallas_call(
        functools.partial(_route_kernel, tm=tm, chunk=chunk),
        out_shape=[
            jax.ShapeDtypeStruct((TOP_K, t), F32),
            jax.ShapeDtypeStruct((TOP_K, t), jnp.int32),
            jax.ShapeDtypeStruct((VREG_SUBLANES, VREG_LANES), jnp.int32),
        ],
        compiler_params=pltpu.CompilerParams(vmem_limit_bytes=V7X_VMEM_LIMIT_BYTES),
        name="moe_route",
    )(logits_t)


def _dispatch_kernel(pos_ref, x_ref, init_hbm, o_hbm, sem, *, tb):
    del init_hbm

    def issue(r, carry):
        for c in range(TOP_K):
            pltpu.make_async_copy(x_ref.at[r], o_hbm.at[pos_ref[c, r]], sem).start()
        return carry

    lax.fori_loop(0, tb, issue, 0)
    for c in range(TOP_K):
        pltpu.make_async_copy(x_ref, o_hbm.at[pl.ds(0, tb)], sem).wait()


def _dispatch(pos, x, m_pad, tb=256):
    t, d = x.shape
    tb = min(tb, t)
    x3 = x.reshape(t, d // VREG_LANES, VREG_LANES)
    out = pl.pallas_call(
        functools.partial(_dispatch_kernel, tb=tb),
        grid=(t // tb,),
        in_specs=[
            pl.BlockSpec((TOP_K, tb), lambda i: (0, i), memory_space=pltpu.SMEM),
            pl.BlockSpec((tb,) + x3.shape[1:], lambda i: (i, 0, 0)),
            pl.BlockSpec(memory_space=pl.ANY),
        ],
        out_specs=pl.BlockSpec(memory_space=pl.ANY),
        out_shape=jax.ShapeDtypeStruct((m_pad,) + x3.shape[1:], x.dtype),
        scratch_shapes=[pltpu.SemaphoreType.DMA(())],
        input_output_aliases={2: 0},
        compiler_params=_params("arbitrary"),
        name="moe_dispatch",
    )(pos, x3, jnp.zeros((m_pad,) + x3.shape[1:], x.dtype))
    return out.reshape(m_pad, d)


META_EXPERT, META_N_USED, META_ROWS = 0, 1, 2


def _clamp_block(b, meta_ref):
    return jnp.minimum(b, meta_ref[META_N_USED, 0] - 1)


def _block_expert(b, meta_ref):
    return meta_ref[META_EXPERT, _clamp_block(b, meta_ref)]


def _sub_blocks(rows, sub):
    return lax.div(rows + (sub - 1), sub)


def _gmm_swiglu_kernel(meta_ref, a_ref, wg_ref, wu_ref, wd_ref, o_ref, wd_bf16_ref, *, tm, sub):
    rows = meta_ref[META_ROWS, pl.program_id(1)]
    n_sub = _sub_blocks(rows, sub)

    def round_down_weights():
        wd_bf16_ref[...] = wd_ref[...].astype(BF16)

    def swiglu_tile(s):
        r0 = pl.multiple_of(s * sub, sub)
        a = a_ref[pl.ds(r0, sub), :]
        g = _dot(a, wg_ref[...].astype(BF16))
        u = _dot(a, wu_ref[...].astype(BF16))
        o_ref[pl.ds(r0, sub), :] = (g * jax.nn.sigmoid(g) * u).astype(o_ref.dtype)

    def zero_tile(s):
        r0 = pl.multiple_of(s * sub, sub)
        o_ref[pl.ds(r0, sub), :] = jnp.zeros((sub,) + o_ref.shape[1:], o_ref.dtype)

    @pl.when(rows > 0)
    def _():
        round_down_weights()
        swiglu_tile(0)

    pl.when(rows == 0)(round_down_weights)
    lax.fori_loop(1, n_sub, lambda s, c: (swiglu_tile(s), c)[1], 0)
    lax.fori_loop(n_sub, tm // sub, lambda s, c: (zero_tile(s), c)[1], 0)


def _cast_chunk_rows(total_rows, n_steps):
    bf16_tile_rows = 2 * VREG_SUBLANES
    for rows in range(bf16_tile_rows, total_rows + 1, bf16_tile_rows):
        if total_rows % rows == 0 and total_rows // rows <= n_steps:
            return rows
    raise ValueError("no chunking of the down weights fits the grid")


def _grouped_swiglu(meta, a, w_gu, w_down, layer, tm, sub, tn=512):
    m_pad, k = a.shape
    f = w_gu.shape[-1] // 2
    tn = min(tn, f)
    nb = m_pad // tm
    rows, d = w_down.shape
    chunk = _cast_chunk_rows(rows, (f // tn) * nb)
    last_chunk = rows // chunk - 1

    def a_map(n, b, meta):
        return (_clamp_block(b, meta), 0)

    def w_map(off):
        return lambda n, b, meta: (layer, _block_expert(b, meta), 0, n + off)

    def chunk_map(n, b, meta):
        return (jnp.minimum(n * nb + b, last_chunk), 0)

    return pl.pallas_call(
        functools.partial(_gmm_swiglu_kernel, tm=tm, sub=sub),
        grid_spec=pltpu.PrefetchScalarGridSpec(
            num_scalar_prefetch=1,
            grid=(f // tn, nb),
            in_specs=[
                pl.BlockSpec((tm, k), a_map),
                pl.BlockSpec((None, None, k, tn), w_map(0)),
                pl.BlockSpec((None, None, k, tn), w_map(f // tn)),
                pl.BlockSpec((chunk, d), chunk_map),
            ],
            out_specs=[
                pl.BlockSpec((tm, tn), lambda n, b, meta: (b, n)),
                pl.BlockSpec((chunk, d), chunk_map),
            ],
        ),
        out_shape=[
            jax.ShapeDtypeStruct((m_pad, f), BF16),
            jax.ShapeDtypeStruct((rows, d), BF16),
        ],
        compiler_params=_params("arbitrary", "arbitrary"),
        name="moe_gu",
    )(meta, a, w_gu, w_gu, w_down)


def _gmm_down_kernel(meta_ref, a_ref, w_ref, o_ref, *, sub):
    n_sub = _sub_blocks(meta_ref[META_ROWS, pl.program_id(0)], sub)

    @pl.when(pl.program_id(1) == 0)
    def _():
        o_ref[...] = jnp.zeros_like(o_ref)

    def accumulate_tile(s, carry):
        r0 = pl.multiple_of(s * sub, sub)
        o_ref[pl.ds(r0, sub), :] += _dot(a_ref[pl.ds(r0, sub), :], w_ref[...])
        return carry

    lax.fori_loop(0, n_sub, accumulate_tile, 0)


def _grouped_down(meta, a, w_down_bf16, tm, sub, tk=1024):
    m_pad, k_dim = a.shape
    n = w_down_bf16.shape[-1]
    tk = min(tk, k_dim)
    nk = k_dim // tk
    nb = m_pad // tm

    def k_eff(b, k, meta):
        return jnp.where(b < meta[META_N_USED, 0], k, nk - 1)

    return pl.pallas_call(
        functools.partial(_gmm_down_kernel, sub=sub),
        grid_spec=pltpu.PrefetchScalarGridSpec(
            num_scalar_prefetch=1,
            grid=(nb, nk),
            in_specs=[
                pl.BlockSpec((tm, tk),
                             lambda b, k, meta: (_clamp_block(b, meta), k_eff(b, k, meta))),
                pl.BlockSpec((tk, n),
                             lambda b, k, meta: (_block_expert(b, meta) * nk + k_eff(b, k, meta),
                                                 0)),
            ],
            out_specs=pl.BlockSpec((tm, n), lambda b, k, meta: (b, 0)),
        ),
        out_shape=jax.ShapeDtypeStruct((m_pad, n), F32),
        compiler_params=_params("arbitrary", "arbitrary"),
        name="moe_down",
    )(meta, a, w_down_bf16)


def _combine_kernel(pos_ref, pos_next_ref, y_hbm, h_ref, w_ref, g_ref, o_ref, buf_ref, sems,
                    *, tb, n_steps):
    i = pl.program_id(0)
    slot = lax.rem(i, 2)

    def gather(p_ref, s):
        def issue(r, carry):
            for c in range(TOP_K):
                pltpu.make_async_copy(y_hbm.at[pl.ds(p_ref[c, r], 1)],
                                      buf_ref.at[s, c, pl.ds(r, 1)], sems.at[s]).start()
            return carry

        lax.fori_loop(0, tb, issue, 0)

    @pl.when(i == 0)
    def _():
        gather(pos_ref, 0)

    @pl.when(i + 1 < n_steps)
    def _():
        gather(pos_next_ref, 1 - slot)

    for c in range(TOP_K):
        pltpu.make_async_copy(y_hbm.at[pl.ds(0, tb)], buf_ref.at[slot, c], sems.at[slot]).wait()
    w = w_ref[...]
    h = h_ref[...] + (w[:, 0:1] * buf_ref[slot, 0] + w[:, 1:2] * buf_ref[slot, 1])
    y = h * lax.rsqrt(jnp.mean(h * h, axis=-1, keepdims=True) + RMS_EPS)
    o_ref[...] = (y * g_ref[...]).astype(o_ref.dtype)


def _combine_norm(pos, y_sorted, h, gate_w, gain, out_dtype, tb=256):
    t, d = h.shape
    tb = min(tb, t)
    n_steps = t // tb
    return pl.pallas_call(
        functools.partial(_combine_kernel, tb=tb, n_steps=n_steps),
        grid=(n_steps,),
        in_specs=[
            pl.BlockSpec((TOP_K, tb), lambda i: (0, i), memory_space=pltpu.SMEM),
            pl.BlockSpec((TOP_K, tb), lambda i: (0, jnp.minimum(i + 1, n_steps - 1)),
                         memory_space=pltpu.SMEM),
            pl.BlockSpec(memory_space=pl.ANY),
            pl.BlockSpec((tb, d), lambda i: (i, 0)),
            pl.BlockSpec((tb, TOP_K), lambda i: (i, 0)),
            pl.BlockSpec((1, d), lambda i: (0, 0)),
        ],
        out_specs=pl.BlockSpec((tb, d), lambda i: (i, 0)),
        out_shape=jax.ShapeDtypeStruct((t, d), out_dtype),
        scratch_shapes=[pltpu.VMEM((2, TOP_K, tb, d), F32), pltpu.SemaphoreType.DMA((2,))],
        compiler_params=_params("arbitrary"),
        name="moe_combine_norm",
    )(pos, pos, y_sorted, h, gate_w, gain.reshape(1, d).astype(F32))


def kernel(x, norm_mix, norm_ffn, w_qkv_a, w_o_a, norm_kv, w_kvf_b, b_f, w_q_b, w_o_b,
           w_gu_dense, w_down_dense, w_router, w_gu_exp, w_down_exp, norm_final):
    batch, seq, d_model = x.shape
    n_heads = b_f.shape[0]
    d_attn = w_o_a.shape[1]
    head_dim = d_attn // n_heads
    n_experts = w_router.shape[-1]
    assert TOP_K == 2 and norm_mix.shape[0] == 2, "kernel is written for the depth-2 trunk"
    t = batch * seq
    h = x.reshape(t, d_model)

    (hn,) = _rmsnorm(h, [norm_mix[0]], BF16)
    qkv = _matmul(hn, w_qkv_a, (0,), 3 * d_attn, BF16, name="mm_qkv_a")
    o = _sb_attention(qkv, batch, seq, n_heads, head_dim)
    h = _matmul(o, w_o_a, (0,), d_model, F32, residual=h, name="mm_o_a")
    (hn,) = _rmsnorm(h, [norm_ffn[0]], BF16)
    act = _matmul_swiglu(hn, w_gu_dense, (0,), name="mm_gu_dense")
    h = _matmul_down(act, w_down_dense, (0,), h, name="mm_down_dense")

    w_f_t = w_kvf_b[:, 2 * d_attn:].T
    hn_q, hn_kv, flogit_t = _rmsnorm(h, [norm_mix[1], norm_kv], BF16, proj_t=w_f_t)
    kv = _matmul(hn_kv, w_kvf_b, (), 2 * d_attn, BF16, name="mm_kv_b")
    q = _matmul(hn_q, w_q_b, (0,), d_attn, BF16, name="mm_q_b")
    cum_log_f = _fgate_cumsum(flogit_t, b_f, batch, seq)
    o = _fox_attention(q, kv, cum_log_f, batch, seq, n_heads, head_dim)
    h = _matmul(o, w_o_b, (0,), d_model, F32, residual=h, name="mm_o_b")

    hn, logits_t = _rmsnorm(h, [norm_ffn[1]], BF16, proj_t=w_router[0].T)
    tm = min(MOE_ROW_TILE, t)
    m_pad = TOP_K * t + n_experts * tm
    gate_w, pos, meta = _route(logits_t, tm)
    x_sorted = _dispatch(pos, hn, m_pad)
    w_down = w_down_exp[0].reshape(-1, d_model)
    sub = min(MOE_SUB_ROWS, tm)
    act, w_down_bf16 = _grouped_swiglu(meta, x_sorted, w_gu_exp, w_down, 0, tm, sub)
    y_sorted = _grouped_down(meta, act, w_down_bf16, tm, sub)
    out = _combine_norm(pos, y_sorted, h, gate_w.T, norm_final, x.dtype)
    return out.reshape(batch, seq, d_model)
```

```python
import functools

import jax
import jax.numpy as jnp
from jax import lax
from jax.experimental import pallas as pl
from jax.experimental.pallas import tpu as pltpu

RMS_EPS = 1e-6
TOP_K = 2

V7X_VMEM_LIMIT_BYTES = 56 * 1024 * 1024

VREG_SUBLANES = 8
VREG_LANES = 128

MOE_ROW_TILE = 1024
MOE_ROW_QUANTUM = 256
ATTN_TQ = 512
ATTN_TK = 256

BF16 = jnp.bfloat16
F32 = jnp.float32


def _params(*semantics):
    return pltpu.CompilerParams(
        dimension_semantics=semantics, vmem_limit_bytes=V7X_VMEM_LIMIT_BYTES)


def _dot(a, b):
    return jnp.dot(a, b, preferred_element_type=F32)


def _dot_nt(a, b):
    return lax.dot_general(a, b, (((1,), (1,)), ((), ())), preferred_element_type=F32)


def _split3(x):
    hi = x.astype(BF16)
    r1 = x - hi.astype(F32)
    mid = r1.astype(BF16)
    lo = (r1 - mid.astype(F32)).astype(BF16)
    return hi, mid, lo


def _log_sigmoid(z):
    return jnp.minimum(z, 0.0) - jnp.log(1.0 + jnp.exp(-jnp.abs(z)))


def _rmsnorm_kernel(*refs, n_gains, has_proj):
    x_ref = refs[0]
    g_refs = refs[1:1 + n_gains]
    pos = 1 + n_gains
    p_ref = refs[pos] if has_proj else None
    pos += int(has_proj)
    o_refs = refs[pos:pos + n_gains]
    po_ref = refs[pos + n_gains] if has_proj else None

    x = x_ref[...]
    y = x * lax.rsqrt(jnp.mean(x * x, axis=-1, keepdims=True) + RMS_EPS)
    for g_ref, o_ref in zip(g_refs, o_refs):
        o_ref[...] = (y * g_ref[...]).astype(o_ref.dtype)
    if has_proj:
        yn = y * g_refs[-1][...]
        yh, ym, _ = _split3(yn)
        ph, pm, _ = _split3(p_ref[...])
        po_ref[...] = _dot_nt(ph, yh) + (_dot_nt(ph, ym) + _dot_nt(pm, yh))


def _rmsnorm(x, gains, out_dtype, proj_t=None, tm=256):
    t, d = x.shape
    tm = min(tm, t)
    n_gains = len(gains)
    has_proj = proj_t is not None
    in_specs = [pl.BlockSpec((tm, d), lambda i: (i, 0))]
    in_specs += [pl.BlockSpec((1, d), lambda i: (0, 0))] * n_gains
    args = [x] + [g.reshape(1, d).astype(F32) for g in gains]
    out_shape = [jax.ShapeDtypeStruct((t, d), out_dtype)] * n_gains
    out_specs = [pl.BlockSpec((tm, d), lambda i: (i, 0))] * n_gains
    if has_proj:
        n = proj_t.shape[0]
        in_specs.append(pl.BlockSpec((n, d), lambda i: (0, 0)))
        args.append(proj_t)
        out_shape.append(jax.ShapeDtypeStruct((n, t), F32))
        out_specs.append(pl.BlockSpec((n, tm), lambda i: (0, i)))
    outs = pl.pallas_call(
        functools.partial(_rmsnorm_kernel, n_gains=n_gains, has_proj=has_proj),
        grid=(t // tm,),
        in_specs=in_specs,
        out_specs=out_specs,
        out_shape=out_shape,
        compiler_params=_params("parallel"),
        name="rmsnorm",
    )(*args)
    return outs


def _mm_plain_kernel(a_ref, w_ref, o_ref):
    o_ref[...] = _dot(a_ref[...], w_ref[...].astype(BF16)).astype(o_ref.dtype)


def _mm_res_kernel(a_ref, w_ref, r_ref, o_ref):
    o_ref[...] = r_ref[...] + _dot(a_ref[...], w_ref[...].astype(BF16))


def _mm_swiglu_kernel(a_ref, wg_ref, wu_ref, o_ref):
    a = a_ref[...]
    g = _dot(a, wg_ref[...].astype(BF16))
    u = _dot(a, wu_ref[...].astype(BF16))
    o_ref[...] = (g * jax.nn.sigmoid(g) * u).astype(o_ref.dtype)


def _w_spec(w, k, tn, prefix, col_block_offset=0):
    lead = (None,) * len(prefix)
    return pl.BlockSpec(
        lead + (k, tn), lambda n, m: tuple(prefix) + (0, n + col_block_offset))


def _matmul(a, w, prefix, n_out, out_dtype, residual=None, tm=2048, tn=512, name="mm"):
    m_dim, k = a.shape
    tm, tn = min(tm, m_dim), min(tn, n_out)
    assert m_dim % tm == 0 and n_out % tn == 0, (m_dim, n_out, tm, tn)
    grid = (n_out // tn, m_dim // tm)
    in_specs = [pl.BlockSpec((tm, k), lambda n, m: (m, 0)), _w_spec(w, k, tn, prefix)]
    args = [a, w]
    kernel = _mm_plain_kernel
    if residual is not None:
        in_specs.append(pl.BlockSpec((tm, tn), lambda n, m: (m, n)))
        args.append(residual)
        kernel = _mm_res_kernel
    return pl.pallas_call(
        kernel,
        grid=grid,
        in_specs=in_specs,
        out_specs=pl.BlockSpec((tm, tn), lambda n, m: (m, n)),
        out_shape=jax.ShapeDtypeStruct((m_dim, n_out), out_dtype),
        compiler_params=_params("parallel", "parallel"),
        name=name,
    )(*args)


def _matmul_swiglu(a, w_gu, prefix, tm=2048, tn=512, name="mm_swiglu"):
    m_dim, k = a.shape
    f = w_gu.shape[-1] // 2
    tm, tn = min(tm, m_dim), min(tn, f)
    grid = (f // tn, m_dim // tm)
    return pl.pallas_call(
        _mm_swiglu_kernel,
        grid=grid,
        in_specs=[
            pl.BlockSpec((tm, k), lambda n, m: (m, 0)),
            _w_spec(w_gu, k, tn, prefix),
            _w_spec(w_gu, k, tn, prefix, col_block_offset=f // tn),
        ],
        out_specs=pl.BlockSpec((tm, tn), lambda n, m: (m, n)),
        out_shape=jax.ShapeDtypeStruct((m_dim, f), BF16),
        compiler_params=_params("parallel", "parallel"),
        name=name,
    )(a, w_gu, w_gu)


def _mm_down_kernel(a_ref, w_ref, r_ref, o_ref):
    @pl.when(pl.program_id(1) == 0)
    def _():
        o_ref[...] = r_ref[...]

    o_ref[...] += _dot(a_ref[...], w_ref[...].astype(BF16))


def _matmul_down(a, w, prefix, residual, tm=1024, tk=512, name="mm_down"):
    m_dim, k_dim = a.shape
    n = w.shape[-1]
    tm, tk = min(tm, m_dim), min(tk, k_dim)
    nk = k_dim // tk
    lead = (None,) * len(prefix)
    return pl.pallas_call(
        _mm_down_kernel,
        grid=(m_dim // tm, nk),
        in_specs=[
            pl.BlockSpec((tm, tk), lambda m, k: (m, k)),
            pl.BlockSpec(lead + (tk, n), lambda m, k: tuple(prefix) + (k, 0)),
            pl.BlockSpec((tm, n), lambda m, k: (m, 0)),
        ],
        out_specs=pl.BlockSpec((tm, n), lambda m, k: (m, 0)),
        out_shape=jax.ShapeDtypeStruct((m_dim, n), F32),
        compiler_params=_params("parallel", "arbitrary"),
        name=name,
    )(a, w, residual)


def _strict_lower_ones(t):
    j = lax.broadcasted_iota(jnp.int32, (t, t), 0)
    s = lax.broadcasted_iota(jnp.int32, (t, t), 1)
    return jnp.where(j > s, 1.0, 0.0).astype(BF16)


def _sb_attn_kernel(q_ref, k_ref, v_ref, o_ref, acc_ref, run_ref, ls_ref, hi_ref, lo_ref,
                    rs_ref, *, tq, tk, scale):
    i = pl.program_id(2)
    r = tq // tk
    tri = _strict_lower_ones(tk)
    row = lax.broadcasted_iota(jnp.int32, (tq, tk), 0)
    col = lax.broadcasted_iota(jnp.int32, (tq, tk), 1)

    def score_stage(j, mask, slot):
        start = pl.multiple_of(j * tk, tk)
        z = _dot_nt(q_ref[...], k_ref[pl.ds(start, tk), :]) * scale
        ls = _log_sigmoid(z)
        l1m = ls - z
        if mask is not None:
            l1m = jnp.where(mask, l1m, 0.0)
            ls = jnp.where(mask, ls, -jnp.inf)
        hi = l1m.astype(BF16)
        ls_ref[slot] = ls
        hi_ref[slot] = hi
        lo_ref[slot] = (l1m - hi.astype(F32)).astype(BF16)
        rs_ref[slot] = jnp.sum(l1m, axis=-1, keepdims=True)

    def value_stage(j, slot):
        start = pl.multiple_of(j * tk, tk)
        suffix = _dot(hi_ref[slot], tri) + _dot(lo_ref[slot], tri)
        w = jnp.exp(ls_ref[slot] + suffix + run_ref[...])
        acc_ref[...] += _dot(w.astype(BF16), v_ref[pl.ds(start, tk), :])
        run_ref[...] += rs_ref[slot]

    acc_ref[...] = jnp.zeros_like(acc_ref)
    run_ref[...] = jnp.zeros_like(run_ref)
    prev = None
    for d in reversed(range(r)):
        score_stage(i * r + d, col + d * tk < row, d)
        if prev is not None:
            value_stage(*prev)
        prev = (i * r + d, d)

    def body(t, carry):
        base = (i - 1 - t) * r
        pending = (base + r, 0)
        for d in reversed(range(r)):
            score_stage(base + d, None, d)
            value_stage(*pending)
            pending = (base + d, d)
        return carry

    lax.fori_loop(0, i, body, 0)
    value_stage(0, 0)
    o_ref[...] = acc_ref[...].astype(o_ref.dtype)


def _sb_attention(qkv, batch, seq, n_heads, head_dim, tq=ATTN_TQ, tk=ATTN_TK):
    tq, tk = min(tq, seq), min(tk, seq)
    nq = seq // tq
    r = tq // tk
    return pl.pallas_call(
        functools.partial(_sb_attn_kernel, tq=tq, tk=tk, scale=head_dim ** -0.5),
        grid=(batch, n_heads, nq),
        in_specs=[
            pl.BlockSpec((tq, head_dim), lambda b, h, i: (b * nq + i, h)),
            pl.BlockSpec((seq, head_dim), lambda b, h, i: (b, n_heads + h)),
            pl.BlockSpec((seq, head_dim), lambda b, h, i: (b, 2 * n_heads + h)),
        ],
        out_specs=pl.BlockSpec((tq, head_dim), lambda b, h, i: (b * nq + i, h)),
        out_shape=jax.ShapeDtypeStruct((batch * seq, n_heads * head_dim), BF16),
        scratch_shapes=[
            pltpu.VMEM((tq, head_dim), F32),
            pltpu.VMEM((tq, 1), F32),
            pltpu.VMEM((r, tq, tk), F32),
            pltpu.VMEM((r, tq, tk), BF16),
            pltpu.VMEM((r, tq, tk), BF16),
            pltpu.VMEM((r, tq, 1), F32),
        ],
        compiler_params=_params("parallel", "parallel", "parallel"),
        name="sb_attention",
    )(qkv, qkv, qkv)


def _fgate_cumsum_kernel(fl_ref, bf_ref, c_ref, *, seq):
    x = _log_sigmoid(fl_ref[...] + bf_ref[...])
    j = lax.broadcasted_iota(jnp.int32, (seq, seq), 0)
    t = lax.broadcasted_iota(jnp.int32, (seq, seq), 1)
    upper = jnp.where(j <= t, 1.0, 0.0).astype(BF16)
    hi, mid, lo = _split3(x)
    c_ref[...] = _dot(hi, upper) + (_dot(mid, upper) + _dot(lo, upper))


def _fgate_cumsum(flogit_t, b_f, batch, seq):
    n_heads = flogit_t.shape[0]
    return pl.pallas_call(
        functools.partial(_fgate_cumsum_kernel, seq=seq),
        grid=(batch,),
        in_specs=[
            pl.BlockSpec((n_heads, seq), lambda b: (0, b)),
            pl.BlockSpec((n_heads, 1), lambda b: (0, 0)),
        ],
        out_specs=pl.BlockSpec((None, n_heads, seq), lambda b: (b, 0, 0)),
        out_shape=jax.ShapeDtypeStruct((batch, n_heads, seq), F32),
        compiler_params=_params("parallel"),
        name="fgate_cumsum",
    )(flogit_t, b_f.reshape(n_heads, 1).astype(F32))


def _fox_attn_kernel(q_ref, k_ref, v_ref, cq_ref, ck_ref, o_ref, acc_ref, bias_ref, red_ref,
                     p_ref, *, tq, tk, scale):
    i = pl.program_id(2)
    r = tq // tk
    row = lax.broadcasted_iota(jnp.int32, (tq, tk), 0)
    col = lax.broadcasted_iota(jnp.int32, (tq, tk), 1)
    band_masks = [col + d * tk <= row for d in range(r)]

    def logits(j, mask):
        start = pl.multiple_of(j * tk, tk)
        s = (_dot_nt(q_ref[...], k_ref[pl.ds(start, tk), :]) * scale
             + bias_ref[...] - ck_ref[:, pl.ds(start, tk)])
        if mask is not None:
            s = jnp.where(mask, s, -jnp.inf)
        return s

    bias_ref[...] = jnp.broadcast_to(cq_ref[...], (tq, tk))
    red_ref[...] = jnp.full((tq, tk), -jnp.inf, F32)

    def max_step(j, mask):
        red_ref[...] = jnp.maximum(red_ref[...], logits(j, mask))

    for d in range(r):
        max_step(i * r + d, band_masks[d])

    def max_body(t, carry):
        for d in range(r):
            max_step(t * r + d, None)
        return carry

    lax.fori_loop(0, i, max_body, 0)
    row_max = jnp.max(red_ref[...], axis=-1, keepdims=True)

    bias_ref[...] = jnp.broadcast_to(cq_ref[...] - row_max, (tq, tk))
    red_ref[...] = jnp.zeros((tq, tk), F32)
    acc_ref[...] = jnp.zeros_like(acc_ref)

    def prob_stage(j, mask, slot):
        p = jnp.exp(logits(j, mask))
        red_ref[...] += p
        p_ref[slot] = p.astype(BF16)

    def value_stage(j, slot):
        start = pl.multiple_of(j * tk, tk)
        acc_ref[...] += _dot(p_ref[slot], v_ref[pl.ds(start, tk), :])

    prev = None
    for d in range(r):
        prob_stage(i * r + d, band_masks[d], d)
        if prev is not None:
            value_stage(*prev)
        prev = (i * r + d, d)

    def pv_body(t, carry):
        base = t * r
        pending = (jnp.where(t == 0, (i + 1) * r - 1, base - 1), r - 1)
        for d in range(r):
            prob_stage(base + d, None, d)
            value_stage(*pending)
            pending = (base + d, d)
        return carry

    lax.fori_loop(0, i, pv_body, 0)
    value_stage(jnp.where(i == 0, r - 1, i * r - 1), r - 1)
    denom = jnp.sum(red_ref[...], axis=-1, keepdims=True)
    o_ref[...] = (acc_ref[...] / denom).astype(o_ref.dtype)


def _fox_attention(q, kv, cum_log_f, batch, seq, n_heads, head_dim, tq=ATTN_TQ, tk=ATTN_TK):
    tq, tk = min(tq, seq), min(tk, seq)
    nq = seq // tq
    r = tq // tk
    cq = cum_log_f.reshape(batch, n_heads, seq, 1)
    ck = cum_log_f.reshape(batch, n_heads, 1, seq)
    return pl.pallas_call(
        functools.partial(_fox_attn_kernel, tq=tq, tk=tk, scale=head_dim ** -0.5),
        grid=(batch, n_heads, nq),
        in_specs=[
            pl.BlockSpec((tq, head_dim), lambda b, h, i: (b * nq + i, h)),
            pl.BlockSpec((seq, head_dim), lambda b, h, i: (b, h)),
            pl.BlockSpec((seq, head_dim), lambda b, h, i: (b, n_heads + h)),
            pl.BlockSpec((None, None, tq, 1), lambda b, h, i: (b, h, i, 0)),
            pl.BlockSpec((None, None, 1, seq), lambda b, h, i: (b, h, 0, 0)),
        ],
        out_specs=pl.BlockSpec((tq, head_dim), lambda b, h, i: (b * nq + i, h)),
        out_shape=jax.ShapeDtypeStruct((batch * seq, n_heads * head_dim), BF16),
        scratch_shapes=[
            pltpu.VMEM((tq, head_dim), F32),
            pltpu.VMEM((tq, tk), F32),
            pltpu.VMEM((tq, tk), F32),
            pltpu.VMEM((r, tq, tk), BF16),
        ],
        compiler_params=_params("parallel", "parallel", "parallel"),
        name="fox_attention",
    )(q, kv, kv, cq, ck)


def _route_kernel(l_ref, w_ref, pos_ref, meta_ref, *, tm, chunk):
    l = l_ref[...]
    n_exp, t = l.shape
    e_idx = lax.broadcasted_iota(jnp.int32, l.shape, 0)
    m1 = jnp.max(l, axis=0, keepdims=True)
    i1 = jnp.min(jnp.where(l == m1, e_idx, n_exp), axis=0, keepdims=True)
    sel1 = e_idx == i1
    l2 = jnp.where(sel1, -jnp.inf, l)
    m2 = jnp.max(l2, axis=0, keepdims=True)
    i2 = jnp.min(jnp.where(l2 == m2, e_idx, n_exp), axis=0, keepdims=True)
    sel2 = e_idx == i2
    e2 = jnp.exp(m2 - m1)
    denom = 1.0 + e2
    w_ref[0:1, :] = 1.0 / denom
    w_ref[1:2, :] = e2 / denom

    s1 = jnp.where(sel1, 1.0, 0.0)
    s2 = jnp.where(sel2, 1.0, 0.0)
    s_any = (s1 + s2).astype(BF16)
    cnt_col = jnp.sum(s1 + s2, axis=1, keepdims=True)
    cnt_row = _dot_nt(jnp.ones((n_exp, t), BF16), s_any)

    def padded(c):
        return jnp.floor((c + (tm - 1)) * (1.0 / tm)) * tm

    ei = lax.broadcasted_iota(jnp.int32, (n_exp, n_exp), 0)
    ej = lax.broadcasted_iota(jnp.int32, (n_exp, n_exp), 1)
    start_col = jnp.sum(jnp.where(ej < ei, padded(cnt_row), 0.0), axis=1, keepdims=True)
    end_col = start_col + padded(cnt_col)

    jj = lax.broadcasted_iota(jnp.int32, (chunk, chunk), 0)
    tt = lax.broadcasted_iota(jnp.int32, (chunk, chunk), 1)
    before = jnp.where(jj < tt, 1.0, 0.0).astype(BF16)
    carry = jnp.zeros((n_exp, 1), F32)
    for c in range(t // chunk):
        sl = slice(c * chunk, (c + 1) * chunk)
        dest = start_col + carry + _dot(s_any[:, sl], before)
        pos_ref[0:1, sl] = jnp.sum(s1[:, sl] * dest, axis=0, keepdims=True).astype(jnp.int32)
        pos_ref[1:2, sl] = jnp.sum(s2[:, sl] * dest, axis=0, keepdims=True).astype(jnp.int32)
        carry = carry + jnp.sum(s1[:, sl] + s2[:, sl], axis=1, keepdims=True)

    lanes = meta_ref.shape[1]
    blk_start = lax.broadcasted_iota(jnp.int32, (n_exp, lanes), 1).astype(F32) * tm
    blk_expert = jnp.sum(jnp.where(end_col <= blk_start, 1.0, 0.0), axis=0, keepdims=True)
    blk_expert = jnp.minimum(blk_expert, n_exp - 1.0)
    n_used = jnp.max(end_col, axis=0, keepdims=True) * (1.0 / tm)
    in_group = (start_col <= blk_start) & (blk_start < end_col)
    blk_rows = jnp.sum(
        jnp.where(in_group, jnp.clip(start_col + cnt_col - blk_start, 0.0, tm), 0.0),
        axis=0, keepdims=True)
    meta_row = lax.broadcasted_iota(jnp.int32, meta_ref.shape, 0)
    meta = jnp.where(meta_row == 0, blk_expert, jnp.where(meta_row == 1, n_used, blk_rows))
    meta_ref[...] = meta.astype(jnp.int32)


def _route(logits_t, tm):
    n_exp, t = logits_t.shape
    assert tm & (tm - 1) == 0, "row tile must be a power of two"
    assert TOP_K * t // tm + n_exp <= VREG_LANES
    chunk = min(1024, t)
    return pl.pallas_call(
        functools.partial(_route_kernel, tm=tm, chunk=chunk),
        out_shape=[
            jax.ShapeDtypeStruct((TOP_K, t), F32),
            jax.ShapeDtypeStruct((TOP_K, t), jnp.int32),
            jax.ShapeDtypeStruct((VREG_SUBLANES, VREG_LANES), jnp.int32),
        ],
        compiler_params=pltpu.CompilerParams(vmem_limit_bytes=V7X_VMEM_LIMIT_BYTES),
        name="moe_route",
    )(logits_t)


def _dispatch_kernel(pos_ref, x_ref, init_hbm, o_hbm, sem, *, tb):
    del init_hbm

    def issue(r, carry):
        for c in range(TOP_K):
            pltpu.make_async_copy(x_ref.at[r], o_hbm.at[pos_ref[c, r]], sem).start()
        return carry

    lax.fori_loop(0, tb, issue, 0)
    for c in range(TOP_K):
        pltpu.make_async_copy(x_ref, o_hbm.at[pl.ds(0, tb)], sem).wait()


def _dispatch(pos, x, m_pad, tb=256):
    t, d = x.shape
    tb = min(tb, t)
    x3 = x.reshape(t, d // VREG_LANES, VREG_LANES)
    out = pl.pallas_call(
        functools.partial(_dispatch_kernel, tb=tb),
        grid=(t // tb,),
        in_specs=[
            pl.BlockSpec((TOP_K, tb), lambda i: (0, i), memory_space=pltpu.SMEM),
            pl.BlockSpec((tb,) + x3.shape[1:], lambda i: (i, 0, 0)),
            pl.BlockSpec(memory_space=pl.ANY),
        ],
        out_specs=pl.BlockSpec(memory_space=pl.ANY),
        out_shape=jax.ShapeDtypeStruct((m_pad,) + x3.shape[1:], x.dtype),
        scratch_shapes=[pltpu.SemaphoreType.DMA(())],
        input_output_aliases={2: 0},
        compiler_params=_params("arbitrary"),
        name="moe_dispatch",
    )(pos, x3, jnp.zeros((m_pad,) + x3.shape[1:], x.dtype))
    return out.reshape(m_pad, d)


META_EXPERT, META_N_USED, META_ROWS = 0, 1, 2


def _clamp_block(b, meta_ref):
    return jnp.minimum(b, meta_ref[META_N_USED, 0] - 1)


def _block_expert(b, meta_ref):
    return meta_ref[META_EXPERT, _clamp_block(b, meta_ref)]


def _row_branches(rows, tm, quantum, branch):
    pl.when(rows == 0)(lambda: branch(0))
    for n in range(quantum, tm + 1, quantum):
        pl.when((rows > n - quantum) & (rows <= n))(functools.partial(branch, n))


def _gmm_swiglu_kernel(meta_ref, a_ref, wg_ref, wu_ref, wd_ref, o_ref, wd_bf16_ref, *, tm,
                       quantum):
    rows = meta_ref[META_ROWS, pl.program_id(1)]

    def branch(n_rows):
        wd_bf16_ref[...] = wd_ref[...].astype(BF16)
        if n_rows:
            a = a_ref[:n_rows]
            g = _dot(a, wg_ref[...].astype(BF16))
            u = _dot(a, wu_ref[...].astype(BF16))
            o_ref[:n_rows] = (g * jax.nn.sigmoid(g) * u).astype(o_ref.dtype)
        if n_rows < tm:
            o_ref[n_rows:] = jnp.zeros((tm - n_rows,) + o_ref.shape[1:], o_ref.dtype)

    _row_branches(rows, tm, quantum, branch)


def _cast_chunk_rows(total_rows, n_steps):
    bf16_tile_rows = 2 * VREG_SUBLANES
    for rows in range(bf16_tile_rows, total_rows + 1, bf16_tile_rows):
        if total_rows % rows == 0 and total_rows // rows <= n_steps:
            return rows
    raise ValueError("no chunking of the down weights fits the grid")


def _grouped_swiglu(meta, a, w_gu, w_down, layer, tm, quantum, tn=512):
    m_pad, k = a.shape
    f = w_gu.shape[-1] // 2
    tn = min(tn, f)
    nb = m_pad // tm
    rows, d = w_down.shape
    chunk = _cast_chunk_rows(rows, (f // tn) * nb)
    last_chunk = rows // chunk - 1

    def a_map(n, b, meta):
        return (_clamp_block(b, meta), 0)

    def w_map(off):
        return lambda n, b, meta: (layer, _block_expert(b, meta), 0, n + off)

    def chunk_map(n, b, meta):
        return (jnp.minimum(n * nb + b, last_chunk), 0)

    return pl.pallas_call(
        functools.partial(_gmm_swiglu_kernel, tm=tm, quantum=quantum),
        grid_spec=pltpu.PrefetchScalarGridSpec(
            num_scalar_prefetch=1,
            grid=(f // tn, nb),
            in_specs=[
                pl.BlockSpec((tm, k), a_map),
                pl.BlockSpec((None, None, k, tn), w_map(0)),
                pl.BlockSpec((None, None, k, tn), w_map(f // tn)),
                pl.BlockSpec((chunk, d), chunk_map),
            ],
            out_specs=[
                pl.BlockSpec((tm, tn), lambda n, b, meta: (b, n)),
                pl.BlockSpec((chunk, d), chunk_map),
            ],
        ),
        out_shape=[
            jax.ShapeDtypeStruct((m_pad, f), BF16),
            jax.ShapeDtypeStruct((rows, d), BF16),
        ],
        compiler_params=_params("arbitrary", "arbitrary"),
        name="moe_gu",
    )(meta, a, w_gu, w_gu, w_down)


def _gmm_down_kernel(meta_ref, a_ref, w_ref, o_ref, *, tm, quantum):
    rows = meta_ref[META_ROWS, pl.program_id(0)]

    @pl.when(pl.program_id(1) == 0)
    def _():
        o_ref[...] = jnp.zeros_like(o_ref)

    def branch(n_rows):
        if n_rows:
            o_ref[:n_rows] += _dot(a_ref[:n_rows], w_ref[...])

    _row_branches(rows, tm, quantum, branch)


def _grouped_down(meta, a, w_down_bf16, tm, quantum, tk=1024):
    m_pad, k_dim = a.shape
    n = w_down_bf16.shape[-1]
    tk = min(tk, k_dim)
    nk = k_dim // tk
    nb = m_pad // tm

    def k_eff(b, k, meta):
        return jnp.where(b < meta[META_N_USED, 0], k, nk - 1)

    return pl.pallas_call(
        functools.partial(_gmm_down_kernel, tm=tm, quantum=quantum),
        grid_spec=pltpu.PrefetchScalarGridSpec(
            num_scalar_prefetch=1,
            grid=(nb, nk),
            in_specs=[
                pl.BlockSpec((tm, tk),
                             lambda b, k, meta: (_clamp_block(b, meta), k_eff(b, k, meta))),
                pl.BlockSpec((tk, n),
                             lambda b, k, meta: (_block_expert(b, meta) * nk + k_eff(b, k, meta),
                                                 0)),
            ],
            out_specs=pl.BlockSpec((tm, n), lambda b, k, meta: (b, 0)),
        ),
        out_shape=jax.ShapeDtypeStruct((m_pad, n), F32),
        compiler_params=_params("arbitrary", "arbitrary"),
        name="moe_down",
    )(meta, a, w_down_bf16)


def _combine_kernel(pos_ref, pos_next_ref, y_hbm, h_ref, w_ref, g_ref, o_ref, buf_ref, sems,
                    *, tb, n_steps):
    i = pl.program_id(0)
    slot = lax.rem(i, 2)

    def gather(p_ref, s):
        def issue(r, carry):
            for c in range(TOP_K):
                pltpu.make_async_copy(y_hbm.at[pl.ds(p_ref[c, r], 1)],
                                      buf_ref.at[s, c, pl.ds(r, 1)], sems.at[s]).start()
            return carry

        lax.fori_loop(0, tb, issue, 0)

    @pl.when(i == 0)
    def _():
        gather(pos_ref, 0)

    @pl.when(i + 1 < n_steps)
    def _():
        gather(pos_next_ref, 1 - slot)

    for c in range(TOP_K):
        pltpu.make_async_copy(y_hbm.at[pl.ds(0, tb)], buf_ref.at[slot, c], sems.at[slot]).wait()
    w = w_ref[...]
    h = h_ref[...] + (w[:, 0:1] * buf_ref[slot, 0] + w[:, 1:2] * buf_ref[slot, 1])
    y = h * lax.rsqrt(jnp.mean(h * h, axis=-1, keepdims=True) + RMS_EPS)
    o_ref[...] = (y * g_ref[...]).astype(o_ref.dtype)


def _combine_norm(pos, y_sorted, h, gate_w, gain, out_dtype, tb=256):
    t, d = h.shape
    tb = min(tb, t)
    n_steps = t // tb
    return pl.pallas_call(
        functools.partial(_combine_kernel, tb=tb, n_steps=n_steps),
        grid=(n_steps,),
        in_specs=[
            pl.BlockSpec((TOP_K, tb), lambda i: (0, i), memory_space=pltpu.SMEM),
            pl.BlockSpec((TOP_K, tb), lambda i: (0, jnp.minimum(i + 1, n_steps - 1)),
                         memory_space=pltpu.SMEM),
            pl.BlockSpec(memory_space=pl.ANY),
            pl.BlockSpec((tb, d), lambda i: (i, 0)),
            pl.BlockSpec((tb, TOP_K), lambda i: (i, 0)),
            pl.BlockSpec((1, d), lambda i: (0, 0)),
        ],
        out_specs=pl.BlockSpec((tb, d), lambda i: (i, 0)),
        out_shape=jax.ShapeDtypeStruct((t, d), out_dtype),
        scratch_shapes=[pltpu.VMEM((2, TOP_K, tb, d), F32), pltpu.SemaphoreType.DMA((2,))],
        compiler_params=_params("arbitrary"),
        name="moe_combine_norm",
    )(pos, pos, y_sorted, h, gate_w, gain.reshape(1, d).astype(F32))


def kernel(x, norm_mix, norm_ffn, w_qkv_a, w_o_a, norm_kv, w_kvf_b, b_f, w_q_b, w_o_b,
           w_gu_dense, w_down_dense, w_router, w_gu_exp, w_down_exp, norm_final):
    batch, seq, d_model = x.shape
    n_heads = b_f.shape[0]
    d_attn = w_o_a.shape[1]
    head_dim = d_attn // n_heads
    n_experts = w_router.shape[-1]
    assert TOP_K == 2 and norm_mix.shape[0] == 2, "kernel is written for the depth-2 trunk"
    t = batch * seq
    h = x.reshape(t, d_model)

    (hn,) = _rmsnorm(h, [norm_mix[0]], BF16)
    qkv = _matmul(hn, w_qkv_a, (0,), 3 * d_attn, BF16, name="mm_qkv_a")
    o = _sb_attention(qkv, batch, seq, n_heads, head_dim)
    h = _matmul(o, w_o_a, (0,), d_model, F32, residual=h, name="mm_o_a")
    (hn,) = _rmsnorm(h, [norm_ffn[0]], BF16)
    act = _matmul_swiglu(hn, w_gu_dense, (0,), name="mm_gu_dense")
    h = _matmul_down(act, w_down_dense, (0,), h, name="mm_down_dense")

    w_f_t = w_kvf_b[:, 2 * d_attn:].T
    hn_q, hn_kv, flogit_t = _rmsnorm(h, [norm_mix[1], norm_kv], BF16, proj_t=w_f_t)
    kv = _matmul(hn_kv, w_kvf_b, (), 2 * d_attn, BF16, name="mm_kv_b")
    q = _matmul(hn_q, w_q_b, (0,), d_attn, BF16, name="mm_q_b")
    cum_log_f = _fgate_cumsum(flogit_t, b_f, batch, seq)
    o = _fox_attention(q, kv, cum_log_f, batch, seq, n_heads, head_dim)
    h = _matmul(o, w_o_b, (0,), d_model, F32, residual=h, name="mm_o_b")

    hn, logits_t = _rmsnorm(h, [norm_ffn[1]], BF16, proj_t=w_router[0].T)
    tm = min(MOE_ROW_TILE, t)
    m_pad = TOP_K * t + n_experts * tm
    gate_w, pos, meta = _route(logits_t, tm)
    x_sorted = _dispatch(pos, hn, m_pad)
    w_down = w_down_exp[0].reshape(-1, d_model)
    quantum = min(MOE_ROW_QUANTUM, tm)
    act, w_down_bf16 = _grouped_swiglu(meta, x_sorted, w_gu_exp, w_down, 0, tm, quantum)
    y_sorted = _grouped_down(meta, act, w_down_bf16, tm, quantum)
    out = _combine_norm(pos, y_sorted, h, gate_w.T, norm_final, x.dtype)
    return out.reshape(batch, seq, d_model)
```

```python
import functools

import jax
import jax.numpy as jnp
from jax import lax
from jax.experimental import pallas as pl
from jax.experimental.pallas import tpu as pltpu

RMS_EPS = 1e-6
TOP_K = 2

V7X_VMEM_LIMIT_BYTES = 56 * 1024 * 1024

VREG_SUBLANES = 8
VREG_LANES = 128

MOE_ROW_TILE = 512
MOE_ROW_QUANTUM = 256
DMA_ISSUE_UNROLL = 8
ATTN_TQ = 512
ATTN_TK = 256

BF16 = jnp.bfloat16
F32 = jnp.float32


def _params(*semantics):
    return pltpu.CompilerParams(
        dimension_semantics=semantics, vmem_limit_bytes=V7X_VMEM_LIMIT_BYTES)


def _dot(a, b):
    return jnp.dot(a, b, preferred_element_type=F32)


def _dot_nt(a, b):
    return lax.dot_general(a, b, (((1,), (1,)), ((), ())), preferred_element_type=F32)


def _split3(x):
    hi = x.astype(BF16)
    r1 = x - hi.astype(F32)
    mid = r1.astype(BF16)
    lo = (r1 - mid.astype(F32)).astype(BF16)
    return hi, mid, lo


def _log_sigmoid(z):
    return jnp.minimum(z, 0.0) - jnp.log(1.0 + jnp.exp(-jnp.abs(z)))


def _rmsnorm_kernel(*refs, n_gains, has_proj):
    x_ref = refs[0]
    g_refs = refs[1:1 + n_gains]
    pos = 1 + n_gains
    p_ref = refs[pos] if has_proj else None
    pos += int(has_proj)
    o_refs = refs[pos:pos + n_gains]
    po_ref = refs[pos + n_gains] if has_proj else None

    x = x_ref[...]
    y = x * lax.rsqrt(jnp.mean(x * x, axis=-1, keepdims=True) + RMS_EPS)
    for g_ref, o_ref in zip(g_refs, o_refs):
        o_ref[...] = (y * g_ref[...]).astype(o_ref.dtype)
    if has_proj:
        yn = y * g_refs[-1][...]
        yh, ym, _ = _split3(yn)
        ph, pm, _ = _split3(p_ref[...])
        po_ref[...] = _dot_nt(ph, yh) + (_dot_nt(ph, ym) + _dot_nt(pm, yh))


def _rmsnorm(x, gains, out_dtype, proj_t=None, tm=256):
    t, d = x.shape
    tm = min(tm, t)
    n_gains = len(gains)
    has_proj = proj_t is not None
    in_specs = [pl.BlockSpec((tm, d), lambda i: (i, 0))]
    in_specs += [pl.BlockSpec((1, d), lambda i: (0, 0))] * n_gains
    args = [x] + [g.reshape(1, d).astype(F32) for g in gains]
    out_shape = [jax.ShapeDtypeStruct((t, d), out_dtype)] * n_gains
    out_specs = [pl.BlockSpec((tm, d), lambda i: (i, 0))] * n_gains
    if has_proj:
        n = proj_t.shape[0]
        in_specs.append(pl.BlockSpec((n, d), lambda i: (0, 0)))
        args.append(proj_t)
        out_shape.append(jax.ShapeDtypeStruct((n, t), F32))
        out_specs.append(pl.BlockSpec((n, tm), lambda i: (0, i)))
    outs = pl.pallas_call(
        functools.partial(_rmsnorm_kernel, n_gains=n_gains, has_proj=has_proj),
        grid=(t // tm,),
        in_specs=in_specs,
        out_specs=out_specs,
        out_shape=out_shape,
        compiler_params=_params("parallel"),
        name="rmsnorm",
    )(*args)
    return outs


def _mm_plain_kernel(a_ref, w_ref, o_ref):
    o_ref[...] = _dot(a_ref[...], w_ref[...].astype(BF16)).astype(o_ref.dtype)


def _mm_res_kernel(a_ref, w_ref, r_ref, o_ref):
    o_ref[...] = r_ref[...] + _dot(a_ref[...], w_ref[...].astype(BF16))


def _mm_swiglu_kernel(a_ref, wg_ref, wu_ref, o_ref):
    a = a_ref[...]
    g = _dot(a, wg_ref[...].astype(BF16))
    u = _dot(a, wu_ref[...].astype(BF16))
    o_ref[...] = (g * jax.nn.sigmoid(g) * u).astype(o_ref.dtype)


def _w_spec(w, k, tn, prefix, col_block_offset=0):
    lead = (None,) * len(prefix)
    return pl.BlockSpec(
        lead + (k, tn), lambda n, m: tuple(prefix) + (0, n + col_block_offset))


def _matmul(a, w, prefix, n_out, out_dtype, residual=None, tm=1024, tn=1024, name="mm"):
    m_dim, k = a.shape
    tm, tn = min(tm, m_dim), min(tn, n_out)
    while n_out % tn:
        tn //= 2
    assert m_dim % tm == 0 and tn % VREG_LANES == 0, (m_dim, n_out, tm, tn)
    grid = (n_out // tn, m_dim // tm)
    in_specs = [pl.BlockSpec((tm, k), lambda n, m: (m, 0)), _w_spec(w, k, tn, prefix)]
    args = [a, w]
    kernel = _mm_plain_kernel
    if residual is not None:
        in_specs.append(pl.BlockSpec((tm, tn), lambda n, m: (m, n)))
        args.append(residual)
        kernel = _mm_res_kernel
    return pl.pallas_call(
        kernel,
        grid=grid,
        in_specs=in_specs,
        out_specs=pl.BlockSpec((tm, tn), lambda n, m: (m, n)),
        out_shape=jax.ShapeDtypeStruct((m_dim, n_out), out_dtype),
        compiler_params=_params("parallel", "parallel"),
        name=name,
    )(*args)


def _matmul_swiglu(a, w_gu, prefix, tm=1024, tn=512, name="mm_swiglu"):
    m_dim, k = a.shape
    f = w_gu.shape[-1] // 2
    tm, tn = min(tm, m_dim), min(tn, f)
    grid = (f // tn, m_dim // tm)
    return pl.pallas_call(
        _mm_swiglu_kernel,
        grid=grid,
        in_specs=[
            pl.BlockSpec((tm, k), lambda n, m: (m, 0)),
            _w_spec(w_gu, k, tn, prefix),
            _w_spec(w_gu, k, tn, prefix, col_block_offset=f // tn),
        ],
        out_specs=pl.BlockSpec((tm, tn), lambda n, m: (m, n)),
        out_shape=jax.ShapeDtypeStruct((m_dim, f), BF16),
        compiler_params=_params("parallel", "parallel"),
        name=name,
    )(a, w_gu, w_gu)


def _mm_down_kernel(a_ref, w_ref, r_ref, o_ref):
    @pl.when(pl.program_id(1) == 0)
    def _():
        o_ref[...] = r_ref[...]

    o_ref[...] += _dot(a_ref[...], w_ref[...].astype(BF16))


def _matmul_down(a, w, prefix, residual, tm=1024, tk=512, name="mm_down"):
    m_dim, k_dim = a.shape
    n = w.shape[-1]
    tm, tk = min(tm, m_dim), min(tk, k_dim)
    nk = k_dim // tk
    lead = (None,) * len(prefix)
    return pl.pallas_call(
        _mm_down_kernel,
        grid=(m_dim // tm, nk),
        in_specs=[
            pl.BlockSpec((tm, tk), lambda m, k: (m, k)),
            pl.BlockSpec(lead + (tk, n), lambda m, k: tuple(prefix) + (k, 0)),
            pl.BlockSpec((tm, n), lambda m, k: (m, 0)),
        ],
        out_specs=pl.BlockSpec((tm, n), lambda m, k: (m, 0)),
        out_shape=jax.ShapeDtypeStruct((m_dim, n), F32),
        compiler_params=_params("parallel", "arbitrary"),
        name=name,
    )(a, w, residual)


def _strict_lower_ones(t):
    j = lax.broadcasted_iota(jnp.int32, (t, t), 0)
    s = lax.broadcasted_iota(jnp.int32, (t, t), 1)
    return jnp.where(j > s, 1.0, 0.0).astype(BF16)


def _sb_attn_kernel(q_ref, k_ref, v_ref, o_ref, acc_ref, run_ref, ls_ref, hi_ref, lo_ref,
                    rs_ref, *, tq, tk, scale):
    i = pl.program_id(2)
    r = tq // tk
    tri = _strict_lower_ones(tk)
    row = lax.broadcasted_iota(jnp.int32, (tq, tk), 0)
    col = lax.broadcasted_iota(jnp.int32, (tq, tk), 1)

    def score_stage(j, mask, slot):
        start = pl.multiple_of(j * tk, tk)
        z = _dot_nt(q_ref[...], k_ref[pl.ds(start, tk), :]) * scale
        ls = _log_sigmoid(z)
        l1m = ls - z
        if mask is not None:
            l1m = jnp.where(mask, l1m, 0.0)
            ls = jnp.where(mask, ls, -jnp.inf)
        hi = l1m.astype(BF16)
        ls_ref[slot] = ls
        hi_ref[slot] = hi
        lo_ref[slot] = (l1m - hi.astype(F32)).astype(BF16)
        rs_ref[slot] = jnp.sum(l1m, axis=-1, keepdims=True)

    def value_stage(j, slot):
        start = pl.multiple_of(j * tk, tk)
        suffix = _dot(hi_ref[slot], tri) + _dot(lo_ref[slot], tri)
        w = jnp.exp(ls_ref[slot] + suffix + run_ref[...])
        acc_ref[...] += _dot(w.astype(BF16), v_ref[pl.ds(start, tk), :])
        run_ref[...] += rs_ref[slot]

    acc_ref[...] = jnp.zeros_like(acc_ref)
    run_ref[...] = jnp.zeros_like(run_ref)
    prev = None
    for d in reversed(range(r)):
        score_stage(i * r + d, col + d * tk < row, d)
        if prev is not None:
            value_stage(*prev)
        prev = (i * r + d, d)

    def body(t, carry):
        base = (i - 1 - t) * r
        pending = (base + r, 0)
        for d in reversed(range(r)):
            score_stage(base + d, None, d)
            value_stage(*pending)
            pending = (base + d, d)
        return carry

    lax.fori_loop(0, i, body, 0)
    value_stage(0, 0)
    o_ref[...] = acc_ref[...].astype(o_ref.dtype)


def _sb_attention(qkv, batch, seq, n_heads, head_dim, tq=ATTN_TQ, tk=ATTN_TK):
    tq, tk = min(tq, seq), min(tk, seq)
    nq = seq // tq
    r = tq // tk
    return pl.pallas_call(
        functools.partial(_sb_attn_kernel, tq=tq, tk=tk, scale=head_dim ** -0.5),
        grid=(batch, n_heads, nq),
        in_specs=[
            pl.BlockSpec((tq, head_dim), lambda b, h, i: (b * nq + i, h)),
            pl.BlockSpec((seq, head_dim), lambda b, h, i: (b, n_heads + h)),
            pl.BlockSpec((seq, head_dim), lambda b, h, i: (b, 2 * n_heads + h)),
        ],
        out_specs=pl.BlockSpec((tq, head_dim), lambda b, h, i: (b * nq + i, h)),
        out_shape=jax.ShapeDtypeStruct((batch * seq, n_heads * head_dim), BF16),
        scratch_shapes=[
            pltpu.VMEM((tq, head_dim), F32),
            pltpu.VMEM((tq, 1), F32),
            pltpu.VMEM((r, tq, tk), F32),
            pltpu.VMEM((r, tq, tk), BF16),
            pltpu.VMEM((r, tq, tk), BF16),
            pltpu.VMEM((r, tq, 1), F32),
        ],
        compiler_params=_params("parallel", "parallel", "parallel"),
        name="sb_attention",
    )(qkv, qkv, qkv)


def _fgate_cumsum_kernel(fl_ref, bf_ref, c_ref, *, seq):
    x = _log_sigmoid(fl_ref[...] + bf_ref[...])
    j = lax.broadcasted_iota(jnp.int32, (seq, seq), 0)
    t = lax.broadcasted_iota(jnp.int32, (seq, seq), 1)
    upper = jnp.where(j <= t, 1.0, 0.0).astype(BF16)
    hi, mid, lo = _split3(x)
    c_ref[...] = _dot(hi, upper) + (_dot(mid, upper) + _dot(lo, upper))


def _fgate_cumsum(flogit_t, b_f, batch, seq):
    n_heads = flogit_t.shape[0]
    return pl.pallas_call(
        functools.partial(_fgate_cumsum_kernel, seq=seq),
        grid=(batch,),
        in_specs=[
            pl.BlockSpec((n_heads, seq), lambda b: (0, b)),
            pl.BlockSpec((n_heads, 1), lambda b: (0, 0)),
        ],
        out_specs=pl.BlockSpec((None, n_heads, seq), lambda b: (b, 0, 0)),
        out_shape=jax.ShapeDtypeStruct((batch, n_heads, seq), F32),
        compiler_params=_params("parallel"),
        name="fgate_cumsum",
    )(flogit_t, b_f.reshape(n_heads, 1).astype(F32))


def _fox_attn_kernel(q_ref, k_ref, v_ref, cq_ref, ck_ref, o_ref, acc_ref, bias_ref, red_ref,
                     p_ref, *, tq, tk, scale):
    i = pl.program_id(2)
    r = tq // tk
    row = lax.broadcasted_iota(jnp.int32, (tq, tk), 0)
    col = lax.broadcasted_iota(jnp.int32, (tq, tk), 1)
    band_masks = [col + d * tk <= row for d in range(r)]

    def logits(j, mask):
        start = pl.multiple_of(j * tk, tk)
        s = (_dot_nt(q_ref[...], k_ref[pl.ds(start, tk), :]) * scale
             + bias_ref[...] - ck_ref[:, pl.ds(start, tk)])
        if mask is not None:
            s = jnp.where(mask, s, -jnp.inf)
        return s

    bias_ref[...] = jnp.broadcast_to(cq_ref[...], (tq, tk))
    red_ref[...] = jnp.full((tq, tk), -jnp.inf, F32)

    def max_step(j, mask):
        red_ref[...] = jnp.maximum(red_ref[...], logits(j, mask))

    for d in range(r):
        max_step(i * r + d, band_masks[d])

    def max_body(t, carry):
        for d in range(r):
            max_step(t * r + d, None)
        return carry

    lax.fori_loop(0, i, max_body, 0)
    row_max = jnp.max(red_ref[...], axis=-1, keepdims=True)

    bias_ref[...] = jnp.broadcast_to(cq_ref[...] - row_max, (tq, tk))
    red_ref[...] = jnp.zeros((tq, tk), F32)
    acc_ref[...] = jnp.zeros_like(acc_ref)

    def prob_stage(j, mask, slot):
        p = jnp.exp(logits(j, mask))
        red_ref[...] += p
        p_ref[slot] = p.astype(BF16)

    def value_stage(j, slot):
        start = pl.multiple_of(j * tk, tk)
        acc_ref[...] += _dot(p_ref[slot], v_ref[pl.ds(start, tk), :])

    prev = None
    for d in range(r):
        prob_stage(i * r + d, band_masks[d], d)
        if prev is not None:
            value_stage(*prev)
        prev = (i * r + d, d)

    def pv_body(t, carry):
        base = t * r
        pending = (jnp.where(t == 0, (i + 1) * r - 1, base - 1), r - 1)
        for d in range(r):
            prob_stage(base + d, None, d)
            value_stage(*pending)
            pending = (base + d, d)
        return carry

    lax.fori_loop(0, i, pv_body, 0)
    value_stage(jnp.where(i == 0, r - 1, i * r - 1), r - 1)
    denom = jnp.sum(red_ref[...], axis=-1, keepdims=True)
    o_ref[...] = (acc_ref[...] / denom).astype(o_ref.dtype)


def _fox_attention(q, kv, cum_log_f, batch, seq, n_heads, head_dim, tq=ATTN_TQ, tk=ATTN_TK):
    tq, tk = min(tq, seq), min(tk, seq)
    nq = seq // tq
    r = tq // tk
    cq = cum_log_f.reshape(batch, n_heads, seq, 1)
    ck = cum_log_f.reshape(batch, n_heads, 1, seq)
    return pl.pallas_call(
        functools.partial(_fox_attn_kernel, tq=tq, tk=tk, scale=head_dim ** -0.5),
        grid=(batch, n_heads, nq),
        in_specs=[
            pl.BlockSpec((tq, head_dim), lambda b, h, i: (b * nq + i, h)),
            pl.BlockSpec((seq, head_dim), lambda b, h, i: (b, h)),
            pl.BlockSpec((seq, head_dim), lambda b, h, i: (b, n_heads + h)),
            pl.BlockSpec((None, None, tq, 1), lambda b, h, i: (b, h, i, 0)),
            pl.BlockSpec((None, None, 1, seq), lambda b, h, i: (b, h, 0, 0)),
        ],
        out_specs=pl.BlockSpec((tq, head_dim), lambda b, h, i: (b * nq + i, h)),
        out_shape=jax.ShapeDtypeStruct((batch * seq, n_heads * head_dim), BF16),
        scratch_shapes=[
            pltpu.VMEM((tq, head_dim), F32),
            pltpu.VMEM((tq, tk), F32),
            pltpu.VMEM((tq, tk), F32),
            pltpu.VMEM((r, tq, tk), BF16),
        ],
        compiler_params=_params("parallel", "parallel", "parallel"),
        name="fox_attention",
    )(q, kv, kv, cq, ck)


def _route_kernel(l_ref, w_ref, pos_ref, meta_ref, *, tm, chunk):
    l = l_ref[...]
    n_exp, t = l.shape
    e_idx = lax.broadcasted_iota(jnp.int32, l.shape, 0)
    m1 = jnp.max(l, axis=0, keepdims=True)
    i1 = jnp.min(jnp.where(l == m1, e_idx, n_exp), axis=0, keepdims=True)
    sel1 = e_idx == i1
    l2 = jnp.where(sel1, -jnp.inf, l)
    m2 = jnp.max(l2, axis=0, keepdims=True)
    i2 = jnp.min(jnp.where(l2 == m2, e_idx, n_exp), axis=0, keepdims=True)
    sel2 = e_idx == i2
    e2 = jnp.exp(m2 - m1)
    denom = 1.0 + e2
    w_ref[0:1, :] = 1.0 / denom
    w_ref[1:2, :] = e2 / denom

    s1 = jnp.where(sel1, 1.0, 0.0)
    s2 = jnp.where(sel2, 1.0, 0.0)
    s_any = (s1 + s2).astype(BF16)
    cnt_col = jnp.sum(s1 + s2, axis=1, keepdims=True)
    cnt_row = _dot_nt(jnp.ones((n_exp, t), BF16), s_any)

    def padded(c):
        return jnp.floor((c + (tm - 1)) * (1.0 / tm)) * tm

    ei = lax.broadcasted_iota(jnp.int32, (n_exp, n_exp), 0)
    ej = lax.broadcasted_iota(jnp.int32, (n_exp, n_exp), 1)
    start_col = jnp.sum(jnp.where(ej < ei, padded(cnt_row), 0.0), axis=1, keepdims=True)
    end_col = start_col + padded(cnt_col)

    jj = lax.broadcasted_iota(jnp.int32, (chunk, chunk), 0)
    tt = lax.broadcasted_iota(jnp.int32, (chunk, chunk), 1)
    before = jnp.where(jj < tt, 1.0, 0.0).astype(BF16)
    carry = jnp.zeros((n_exp, 1), F32)
    for c in range(t // chunk):
        sl = slice(c * chunk, (c + 1) * chunk)
        dest = start_col + carry + _dot(s_any[:, sl], before)
        pos_ref[0:1, sl] = jnp.sum(s1[:, sl] * dest, axis=0, keepdims=True).astype(jnp.int32)
        pos_ref[1:2, sl] = jnp.sum(s2[:, sl] * dest, axis=0, keepdims=True).astype(jnp.int32)
        carry = carry + jnp.sum(s1[:, sl] + s2[:, sl], axis=1, keepdims=True)

    lanes = meta_ref.shape[1]
    blk_start = lax.broadcasted_iota(jnp.int32, (n_exp, lanes), 1).astype(F32) * tm
    blk_expert = jnp.sum(jnp.where(end_col <= blk_start, 1.0, 0.0), axis=0, keepdims=True)
    blk_expert = jnp.minimum(blk_expert, n_exp - 1.0)
    n_used = jnp.max(end_col, axis=0, keepdims=True) * (1.0 / tm)
    in_group = (start_col <= blk_start) & (blk_start < end_col)
    blk_rows = jnp.sum(
        jnp.where(in_group, jnp.clip(start_col + cnt_col - blk_start, 0.0, tm), 0.0),
        axis=0, keepdims=True)
    meta_row = lax.broadcasted_iota(jnp.int32, meta_ref.shape, 0)
    meta = jnp.where(meta_row == 0, blk_expert, jnp.where(meta_row == 1, n_used, blk_rows))
    meta_ref[...] = meta.astype(jnp.int32)


def _route(logits_t, tm):
    n_exp, t = logits_t.shape
    assert tm & (tm - 1) == 0, "row tile must be a power of two"
    assert TOP_K * t // tm + n_exp <= VREG_LANES
    chunk = min(1024, t)
    return pl.pallas_call(
        functools.partial(_route_kernel, tm=tm, chunk=chunk),
        out_shape=[
            jax.ShapeDtypeStruct((TOP_K, t), F32),
            jax.ShapeDtypeStruct((TOP_K, t), jnp.int32),
            jax.ShapeDtypeStruct((VREG_SUBLANES, VREG_LANES), jnp.int32),
        ],
        compiler_params=pltpu.CompilerParams(vmem_limit_bytes=V7X_VMEM_LIMIT_BYTES),
        name="moe_route",
    )(logits_t)


def _dispatch_kernel(pos_ref, x_ref, init_hbm, o_hbm, sem, *, tb):
    del init_hbm

    def issue(r, carry):
        for c in range(TOP_K):
            pltpu.make_async_copy(x_ref.at[r], o_hbm.at[pos_ref[c, r]], sem).start()
        return carry

    lax.fori_loop(0, tb, issue, 0, unroll=DMA_ISSUE_UNROLL)
    for c in range(TOP_K):
        pltpu.make_async_copy(x_ref, o_hbm.at[pl.ds(0, tb)], sem).wait()


def _dispatch(pos, x, m_pad, tb=256):
    t, d = x.shape
    tb = min(tb, t)
    x3 = x.reshape(t, d // VREG_LANES, VREG_LANES)
    out = pl.pallas_call(
        functools.partial(_dispatch_kernel, tb=tb),
        grid=(t // tb,),
        in_specs=[
            pl.BlockSpec((TOP_K, tb), lambda i: (0, i), memory_space=pltpu.SMEM),
            pl.BlockSpec((tb,) + x3.shape[1:], lambda i: (i, 0, 0)),
            pl.BlockSpec(memory_space=pl.ANY),
        ],
        out_specs=pl.BlockSpec(memory_space=pl.ANY),
        out_shape=jax.ShapeDtypeStruct((m_pad,) + x3.shape[1:], x.dtype),
        scratch_shapes=[pltpu.SemaphoreType.DMA(())],
        input_output_aliases={2: 0},
        compiler_params=_params("arbitrary"),
        name="moe_dispatch",
    )(pos, x3, jnp.zeros((m_pad,) + x3.shape[1:], x.dtype))
    return out.reshape(m_pad, d)


META_EXPERT, META_N_USED, META_ROWS = 0, 1, 2


def _clamp_block(b, meta_ref):
    return jnp.minimum(b, meta_ref[META_N_USED, 0] - 1)


def _block_expert(b, meta_ref):
    return meta_ref[META_EXPERT, _clamp_block(b, meta_ref)]


def _row_branches(rows, tm, quantum, branch):
    pl.when(rows == 0)(lambda: branch(0))
    for n in range(quantum, tm + 1, quantum):
        pl.when((rows > n - quantum) & (rows <= n))(functools.partial(branch, n))


def _gmm_swiglu_kernel(meta_ref, a_ref, wg_ref, wu_ref, wd_ref, o_ref, wd_bf16_ref, *, tm,
                       quantum):
    rows = meta_ref[META_ROWS, pl.program_id(1)]

    def branch(n_rows):
        wd_bf16_ref[...] = wd_ref[...].astype(BF16)
        if n_rows:
            a = a_ref[:n_rows]
            g = _dot(a, wg_ref[...].astype(BF16))
            u = _dot(a, wu_ref[...].astype(BF16))
            o_ref[:n_rows] = (g * jax.nn.sigmoid(g) * u).astype(o_ref.dtype)
        if n_rows < tm:
            o_ref[n_rows:] = jnp.zeros((tm - n_rows,) + o_ref.shape[1:], o_ref.dtype)

    _row_branches(rows, tm, quantum, branch)


def _cast_chunk_rows(total_rows, n_steps):
    bf16_tile_rows = 2 * VREG_SUBLANES
    for rows in range(bf16_tile_rows, total_rows + 1, bf16_tile_rows):
        if total_rows % rows == 0 and total_rows // rows <= n_steps:
            return rows
    raise ValueError("no chunking of the down weights fits the grid")


def _grouped_swiglu(meta, a, w_gu, w_down, layer, tm, quantum, tn=1024):
    m_pad, k = a.shape
    f = w_gu.shape[-1] // 2
    tn = min(tn, f)
    nb = m_pad // tm
    rows, d = w_down.shape
    chunk = _cast_chunk_rows(rows, (f // tn) * nb)
    last_chunk = rows // chunk - 1

    def a_map(n, b, meta):
        return (_clamp_block(b, meta), 0)

    def w_map(off):
        return lambda n, b, meta: (layer, _block_expert(b, meta), 0, n + off)

    def chunk_map(n, b, meta):
        return (jnp.minimum(n * nb + b, last_chunk), 0)

    return pl.pallas_call(
        functools.partial(_gmm_swiglu_kernel, tm=tm, quantum=quantum),
        grid_spec=pltpu.PrefetchScalarGridSpec(
            num_scalar_prefetch=1,
            grid=(f // tn, nb),
            in_specs=[
                pl.BlockSpec((tm, k), a_map),
                pl.BlockSpec((None, None, k, tn), w_map(0)),
                pl.BlockSpec((None, None, k, tn), w_map(f // tn)),
                pl.BlockSpec((chunk, d), chunk_map),
            ],
            out_specs=[
                pl.BlockSpec((tm, tn), lambda n, b, meta: (b, n)),
                pl.BlockSpec((chunk, d), chunk_map),
            ],
        ),
        out_shape=[
            jax.ShapeDtypeStruct((m_pad, f), BF16),
            jax.ShapeDtypeStruct((rows, d), BF16),
        ],
        compiler_params=_params("arbitrary", "arbitrary"),
        name="moe_gu",
    )(meta, a, w_gu, w_gu, w_down)


def _gmm_down_kernel(meta_ref, a_ref, w_ref, o_ref, *, tm, quantum):
    rows = meta_ref[META_ROWS, pl.program_id(0)]

    @pl.when(pl.program_id(1) == 0)
    def _():
        o_ref[...] = jnp.zeros_like(o_ref)

    def branch(n_rows):
        if n_rows:
            o_ref[:n_rows] += _dot(a_ref[:n_rows], w_ref[...])

    _row_branches(rows, tm, quantum, branch)


def _grouped_down(meta, a, w_down_bf16, tm, quantum, tk=1024):
    m_pad, k_dim = a.shape
    n = w_down_bf16.shape[-1]
    tk = min(tk, k_dim)
    nk = k_dim // tk
    nb = m_pad // tm

    def k_eff(b, k, meta):
        return jnp.where(b < meta[META_N_USED, 0], k, nk - 1)

    return pl.pallas_call(
        functools.partial(_gmm_down_kernel, tm=tm, quantum=quantum),
        grid_spec=pltpu.PrefetchScalarGridSpec(
            num_scalar_prefetch=1,
            grid=(nb, nk),
            in_specs=[
                pl.BlockSpec((tm, tk),
                             lambda b, k, meta: (_clamp_block(b, meta), k_eff(b, k, meta))),
                pl.BlockSpec((tk, n),
                             lambda b, k, meta: (_block_expert(b, meta) * nk + k_eff(b, k, meta),
                                                 0)),
            ],
            out_specs=pl.BlockSpec((tm, n), lambda b, k, meta: (b, 0)),
        ),
        out_shape=jax.ShapeDtypeStruct((m_pad, n), F32),
        compiler_params=_params("arbitrary", "arbitrary"),
        name="moe_down",
    )(meta, a, w_down_bf16)


def _combine_kernel(pos_ref, pos_next_ref, y_hbm, h_ref, w_ref, g_ref, o_ref, buf_ref, sems,
                    *, tb, n_steps):
    i = pl.program_id(0)
    slot = lax.rem(i, 2)

    def gather(p_ref, s):
        def issue(r, carry):
            for c in range(TOP_K):
                pltpu.make_async_copy(y_hbm.at[pl.ds(p_ref[c, r], 1)],
                                      buf_ref.at[s, c, pl.ds(r, 1)], sems.at[s]).start()
            return carry

        lax.fori_loop(0, tb, issue, 0, unroll=DMA_ISSUE_UNROLL)

    @pl.when(i == 0)
    def _():
        gather(pos_ref, 0)

    @pl.when(i + 1 < n_steps)
    def _():
        gather(pos_next_ref, 1 - slot)

    for c in range(TOP_K):
        pltpu.make_async_copy(y_hbm.at[pl.ds(0, tb)], buf_ref.at[slot, c], sems.at[slot]).wait()
    w = w_ref[...]
    h = h_ref[...] + (w[:, 0:1] * buf_ref[slot, 0] + w[:, 1:2] * buf_ref[slot, 1])
    y = h * lax.rsqrt(jnp.mean(h * h, axis=-1, keepdims=True) + RMS_EPS)
    o_ref[...] = (y * g_ref[...]).astype(o_ref.dtype)


def _combine_norm(pos, y_sorted, h, gate_w, gain, out_dtype, tb=256):
    t, d = h.shape
    tb = min(tb, t)
    n_steps = t // tb
    return pl.pallas_call(
        functools.partial(_combine_kernel, tb=tb, n_steps=n_steps),
        grid=(n_steps,),
        in_specs=[
            pl.BlockSpec((TOP_K, tb), lambda i: (0, i), memory_space=pltpu.SMEM),
            pl.BlockSpec((TOP_K, tb), lambda i: (0, jnp.minimum(i + 1, n_steps - 1)),
                         memory_space=pltpu.SMEM),
            pl.BlockSpec(memory_space=pl.ANY),
            pl.BlockSpec((tb, d), lambda i: (i, 0)),
            pl.BlockSpec((tb, TOP_K), lambda i: (i, 0)),
            pl.BlockSpec((1, d), lambda i: (0, 0)),
        ],
        out_specs=pl.BlockSpec((tb, d), lambda i: (i, 0)),
        out_shape=jax.ShapeDtypeStruct((t, d), out_dtype),
        scratch_shapes=[pltpu.VMEM((2, TOP_K, tb, d), F32), pltpu.SemaphoreType.DMA((2,))],
        compiler_params=_params("arbitrary"),
        name="moe_combine_norm",
    )(pos, pos, y_sorted, h, gate_w, gain.reshape(1, d).astype(F32))


def kernel(x, norm_mix, norm_ffn, w_qkv_a, w_o_a, norm_kv, w_kvf_b, b_f, w_q_b, w_o_b,
           w_gu_dense, w_down_dense, w_router, w_gu_exp, w_down_exp, norm_final):
    batch, seq, d_model = x.shape
    n_heads = b_f.shape[0]
    d_attn = w_o_a.shape[1]
    head_dim = d_attn // n_heads
    n_experts = w_router.shape[-1]
    assert TOP_K == 2 and norm_mix.shape[0] == 2, "kernel is written for the depth-2 trunk"
    t = batch * seq
    h = x.reshape(t, d_model)

    (hn,) = _rmsnorm(h, [norm_mix[0]], BF16)
    qkv = _matmul(hn, w_qkv_a, (0,), 3 * d_attn, BF16, name="mm_qkv_a")
    o = _sb_attention(qkv, batch, seq, n_heads, head_dim)
    h = _matmul(o, w_o_a, (0,), d_model, F32, residual=h, name="mm_o_a")
    (hn,) = _rmsnorm(h, [norm_ffn[0]], BF16)
    act = _matmul_swiglu(hn, w_gu_dense, (0,), name="mm_gu_dense")
    h = _matmul_down(act, w_down_dense, (0,), h, name="mm_down_dense")

    w_f_t = w_kvf_b[:, 2 * d_attn:].T
    hn_q, hn_kv, flogit_t = _rmsnorm(h, [norm_mix[1], norm_kv], BF16, proj_t=w_f_t)
    kv = _matmul(hn_kv, w_kvf_b, (), 2 * d_attn, BF16, name="mm_kv_b")
    q = _matmul(hn_q, w_q_b, (0,), d_attn, BF16, name="mm_q_b")
    cum_log_f = _fgate_cumsum(flogit_t, b_f, batch, seq)
    o = _fox_attention(q, kv, cum_log_f, batch, seq, n_heads, head_dim)
    h = _matmul(o, w_o_b, (0,), d_model, F32, residual=h, name="mm_o_b")

    hn, logits_t = _rmsnorm(h, [norm_ffn[1]], BF16, proj_t=w_router[0].T)
    tm = min(MOE_ROW_TILE, t)
    m_pad = TOP_K * t + n_experts * tm
    gate_w, pos, meta = _route(logits_t, tm)
    x_sorted = _dispatch(pos, hn, m_pad)
    w_down = w_down_exp[0].reshape(-1, d_model)
    quantum = min(MOE_ROW_QUANTUM, tm)
    act, w_down_bf16 = _grouped_swiglu(meta, x_sorted, w_gu_exp, w_down, 0, tm, quantum)
    y_sorted = _grouped_down(meta, act, w_down_bf16, tm, quantum)
    out = _combine_norm(pos, y_sorted, h, gate_w.T, norm_final, x.dtype)
    return out.reshape(batch, seq, d_model)
```

```python
import functools

import jax
import jax.numpy as jnp
from jax import lax
from jax.experimental import pallas as pl
from jax.experimental.pallas import tpu as pltpu

RMS_EPS = 1e-6
TOP_K = 2

V7X_VMEM_LIMIT_BYTES = 56 * 1024 * 1024

VREG_SUBLANES = 8
VREG_LANES = 128

MOE_ROW_TILE = 512
MOE_ROW_QUANTUM = 256
DMA_ISSUE_UNROLL = 8
ATTN_TQ = 512
ATTN_TK = 256

BF16 = jnp.bfloat16
F32 = jnp.float32


def _params(*semantics):
    return pltpu.CompilerParams(
        dimension_semantics=semantics, vmem_limit_bytes=V7X_VMEM_LIMIT_BYTES)


def _dot(a, b):
    return jnp.dot(a, b, preferred_element_type=F32)


def _dot_nt(a, b):
    return lax.dot_general(a, b, (((1,), (1,)), ((), ())), preferred_element_type=F32)


def _split3(x):
    hi = x.astype(BF16)
    r1 = x - hi.astype(F32)
    mid = r1.astype(BF16)
    lo = (r1 - mid.astype(F32)).astype(BF16)
    return hi, mid, lo


def _log_sigmoid(z):
    return jnp.minimum(z, 0.0) - jnp.log(1.0 + jnp.exp(-jnp.abs(z)))


def _rmsnorm_kernel(*refs, n_gains, has_proj):
    x_ref = refs[0]
    g_refs = refs[1:1 + n_gains]
    pos = 1 + n_gains
    p_ref = refs[pos] if has_proj else None
    pos += int(has_proj)
    o_refs = refs[pos:pos + n_gains]
    po_ref = refs[pos + n_gains] if has_proj else None

    x = x_ref[...]
    y = x * lax.rsqrt(jnp.mean(x * x, axis=-1, keepdims=True) + RMS_EPS)
    for g_ref, o_ref in zip(g_refs, o_refs):
        o_ref[...] = (y * g_ref[...]).astype(o_ref.dtype)
    if has_proj:
        yn = y * g_refs[-1][...]
        yh, ym, _ = _split3(yn)
        ph, pm, _ = _split3(p_ref[...])
        po_ref[...] = _dot_nt(ph, yh) + (_dot_nt(ph, ym) + _dot_nt(pm, yh))


def _rmsnorm(x, gains, out_dtype, proj_t=None, tm=256):
    t, d = x.shape
    tm = min(tm, t)
    n_gains = len(gains)
    has_proj = proj_t is not None
    in_specs = [pl.BlockSpec((tm, d), lambda i: (i, 0))]
    in_specs += [pl.BlockSpec((1, d), lambda i: (0, 0))] * n_gains
    args = [x] + [g.reshape(1, d).astype(F32) for g in gains]
    out_shape = [jax.ShapeDtypeStruct((t, d), out_dtype)] * n_gains
    out_specs = [pl.BlockSpec((tm, d), lambda i: (i, 0))] * n_gains
    if has_proj:
        n = proj_t.shape[0]
        in_specs.append(pl.BlockSpec((n, d), lambda i: (0, 0)))
        args.append(proj_t)
        out_shape.append(jax.ShapeDtypeStruct((n, t), F32))
        out_specs.append(pl.BlockSpec((n, tm), lambda i: (0, i)))
    outs = pl.pallas_call(
        functools.partial(_rmsnorm_kernel, n_gains=n_gains, has_proj=has_proj),
        grid=(t // tm,),
        in_specs=in_specs,
        out_specs=out_specs,
        out_shape=out_shape,
        compiler_params=_params("parallel"),
        name="rmsnorm",
    )(*args)
    return outs


def _mm_plain_kernel(a_ref, w_ref, o_ref):
    o_ref[...] = _dot(a_ref[...], w_ref[...].astype(BF16)).astype(o_ref.dtype)


def _mm_res_kernel(a_ref, w_ref, r_ref, o_ref):
    o_ref[...] = r_ref[...] + _dot(a_ref[...], w_ref[...].astype(BF16))


def _mm_swiglu_kernel(a_ref, wg_ref, wu_ref, o_ref):
    a = a_ref[...]
    g = _dot(a, wg_ref[...].astype(BF16))
    u = _dot(a, wu_ref[...].astype(BF16))
    o_ref[...] = (g * jax.nn.sigmoid(g) * u).astype(o_ref.dtype)


def _w_spec(w, k, tn, prefix, col_block_offset=0):
    lead = (None,) * len(prefix)
    return pl.BlockSpec(
        lead + (k, tn), lambda n, m: tuple(prefix) + (0, n + col_block_offset))


def _matmul(a, w, prefix, n_out, out_dtype, residual=None, tm=1024, tn=1024, name="mm"):
    m_dim, k = a.shape
    tm, tn = min(tm, m_dim), min(tn, n_out)
    while n_out % tn:
        tn //= 2
    assert m_dim % tm == 0 and tn % VREG_LANES == 0, (m_dim, n_out, tm, tn)
    grid = (n_out // tn, m_dim // tm)
    in_specs = [pl.BlockSpec((tm, k), lambda n, m: (m, 0)), _w_spec(w, k, tn, prefix)]
    args = [a, w]
    kernel = _mm_plain_kernel
    if residual is not None:
        in_specs.append(pl.BlockSpec((tm, tn), lambda n, m: (m, n)))
        args.append(residual)
        kernel = _mm_res_kernel
    return pl.pallas_call(
        kernel,
        grid=grid,
        in_specs=in_specs,
        out_specs=pl.BlockSpec((tm, tn), lambda n, m: (m, n)),
        out_shape=jax.ShapeDtypeStruct((m_dim, n_out), out_dtype),
        compiler_params=_params("parallel", "parallel"),
        name=name,
    )(*args)


def _matmul_swiglu(a, w_gu, prefix, tm=1024, tn=512, name="mm_swiglu"):
    m_dim, k = a.shape
    f = w_gu.shape[-1] // 2
    tm, tn = min(tm, m_dim), min(tn, f)
    grid = (f // tn, m_dim // tm)
    return pl.pallas_call(
        _mm_swiglu_kernel,
        grid=grid,
        in_specs=[
            pl.BlockSpec((tm, k), lambda n, m: (m, 0)),
            _w_spec(w_gu, k, tn, prefix),
            _w_spec(w_gu, k, tn, prefix, col_block_offset=f // tn),
        ],
        out_specs=pl.BlockSpec((tm, tn), lambda n, m: (m, n)),
        out_shape=jax.ShapeDtypeStruct((m_dim, f), BF16),
        compiler_params=_params("parallel", "parallel"),
        name=name,
    )(a, w_gu, w_gu)


def _mm_down_kernel(a_ref, w_ref, r_ref, o_ref):
    @pl.when(pl.program_id(1) == 0)
    def _():
        o_ref[...] = r_ref[...]

    o_ref[...] += _dot(a_ref[...], w_ref[...].astype(BF16))


def _matmul_down(a, w, prefix, residual, tm=1024, tk=512, name="mm_down"):
    m_dim, k_dim = a.shape
    n = w.shape[-1]
    tm, tk = min(tm, m_dim), min(tk, k_dim)
    nk = k_dim // tk
    lead = (None,) * len(prefix)
    return pl.pallas_call(
        _mm_down_kernel,
        grid=(m_dim // tm, nk),
        in_specs=[
            pl.BlockSpec((tm, tk), lambda m, k: (m, k)),
            pl.BlockSpec(lead + (tk, n), lambda m, k: tuple(prefix) + (k, 0)),
            pl.BlockSpec((tm, n), lambda m, k: (m, 0)),
        ],
        out_specs=pl.BlockSpec((tm, n), lambda m, k: (m, 0)),
        out_shape=jax.ShapeDtypeStruct((m_dim, n), F32),
        compiler_params=_params("parallel", "arbitrary"),
        name=name,
    )(a, w, residual)


def _strict_lower_ones(t):
    j = lax.broadcasted_iota(jnp.int32, (t, t), 0)
    s = lax.broadcasted_iota(jnp.int32, (t, t), 1)
    return jnp.where(j > s, 1.0, 0.0).astype(BF16)


def _sb_attn_kernel(q_ref, k_ref, v_ref, o_ref, acc_ref, run_ref, ls_ref, hi_ref, lo_ref,
                    rs_ref, *, tq, tk, scale):
    i = pl.program_id(2)
    r = tq // tk
    tri = _strict_lower_ones(tk)
    row = lax.broadcasted_iota(jnp.int32, (tq, tk), 0)
    col = lax.broadcasted_iota(jnp.int32, (tq, tk), 1)

    def score_stage(j, mask, slot):
        start = pl.multiple_of(j * tk, tk)
        z = _dot_nt(q_ref[...], k_ref[pl.ds(start, tk), :]) * scale
        ls = _log_sigmoid(z)
        l1m = ls - z
        if mask is not None:
            l1m = jnp.where(mask, l1m, 0.0)
            ls = jnp.where(mask, ls, -jnp.inf)
        hi = l1m.astype(BF16)
        ls_ref[slot] = ls
        hi_ref[slot] = hi
        lo_ref[slot] = (l1m - hi.astype(F32)).astype(BF16)
        rs_ref[slot] = jnp.sum(l1m, axis=-1, keepdims=True)

    def value_stage(j, slot):
        start = pl.multiple_of(j * tk, tk)
        suffix = _dot(hi_ref[slot], tri) + _dot(lo_ref[slot], tri)
        w = jnp.exp(ls_ref[slot] + suffix + run_ref[...])
        acc_ref[...] += _dot(w.astype(BF16), v_ref[pl.ds(start, tk), :])
        run_ref[...] += rs_ref[slot]

    acc_ref[...] = jnp.zeros_like(acc_ref)
    run_ref[...] = jnp.zeros_like(run_ref)
    prev = None
    for d in reversed(range(r)):
        score_stage(i * r + d, col + d * tk < row, d)
        if prev is not None:
            value_stage(*prev)
        prev = (i * r + d, d)

    def body(t, carry):
        base = (i - 1 - t) * r
        pending = (base + r, 0)
        for d in reversed(range(r)):
            score_stage(base + d, None, d)
            value_stage(*pending)
            pending = (base + d, d)
        return carry

    lax.fori_loop(0, i, body, 0)
    value_stage(0, 0)
    o_ref[...] = acc_ref[...].astype(o_ref.dtype)


def _sb_attention(qkv, batch, seq, n_heads, head_dim, tq=ATTN_TQ, tk=ATTN_TK):
    tq, tk = min(tq, seq), min(tk, seq)
    nq = seq // tq
    r = tq // tk
    return pl.pallas_call(
        functools.partial(_sb_attn_kernel, tq=tq, tk=tk, scale=head_dim ** -0.5),
        grid=(batch, n_heads, nq),
        in_specs=[
            pl.BlockSpec((tq, head_dim), lambda b, h, i: (b * nq + i, h)),
            pl.BlockSpec((seq, head_dim), lambda b, h, i: (b, n_heads + h)),
            pl.BlockSpec((seq, head_dim), lambda b, h, i: (b, 2 * n_heads + h)),
        ],
        out_specs=pl.BlockSpec((tq, head_dim), lambda b, h, i: (b * nq + i, h)),
        out_shape=jax.ShapeDtypeStruct((batch * seq, n_heads * head_dim), BF16),
        scratch_shapes=[
            pltpu.VMEM((tq, head_dim), F32),
            pltpu.VMEM((tq, 1), F32),
            pltpu.VMEM((r, tq, tk), F32),
            pltpu.VMEM((r, tq, tk), BF16),
            pltpu.VMEM((r, tq, tk), BF16),
            pltpu.VMEM((r, tq, 1), F32),
        ],
        compiler_params=_params("parallel", "parallel", "parallel"),
        name="sb_attention",
    )(qkv, qkv, qkv)


def _fgate_cumsum_kernel(fl_ref, bf_ref, c_ref, *, seq):
    x = _log_sigmoid(fl_ref[...] + bf_ref[...])
    j = lax.broadcasted_iota(jnp.int32, (seq, seq), 0)
    t = lax.broadcasted_iota(jnp.int32, (seq, seq), 1)
    upper = jnp.where(j <= t, 1.0, 0.0).astype(BF16)
    hi, mid, lo = _split3(x)
    c_ref[...] = _dot(hi, upper) + (_dot(mid, upper) + _dot(lo, upper))


def _fgate_cumsum(flogit_t, b_f, batch, seq):
    n_heads = flogit_t.shape[0]
    return pl.pallas_call(
        functools.partial(_fgate_cumsum_kernel, seq=seq),
        grid=(batch,),
        in_specs=[
            pl.BlockSpec((n_heads, seq), lambda b: (0, b)),
            pl.BlockSpec((n_heads, 1), lambda b: (0, 0)),
        ],
        out_specs=pl.BlockSpec((None, n_heads, seq), lambda b: (b, 0, 0)),
        out_shape=jax.ShapeDtypeStruct((batch, n_heads, seq), F32),
        compiler_params=_params("parallel"),
        name="fgate_cumsum",
    )(flogit_t, b_f.reshape(n_heads, 1).astype(F32))


def _fox_attn_kernel(q_ref, k_ref, v_ref, cq_ref, ck_ref, o_ref, acc_ref, bias_ref, red_ref,
                     p_ref, *, tq, tk, scale):
    i = pl.program_id(2)
    r = tq // tk
    row = lax.broadcasted_iota(jnp.int32, (tq, tk), 0)
    col = lax.broadcasted_iota(jnp.int32, (tq, tk), 1)
    band_masks = [col + d * tk <= row for d in range(r)]

    def logits(j, mask):
        start = pl.multiple_of(j * tk, tk)
        s = (_dot_nt(q_ref[...], k_ref[pl.ds(start, tk), :]) * scale
             + bias_ref[...] - ck_ref[:, pl.ds(start, tk)])
        if mask is not None:
            s = jnp.where(mask, s, -jnp.inf)
        return s

    bias_ref[...] = jnp.broadcast_to(cq_ref[...], (tq, tk))
    red_ref[...] = jnp.full((tq, tk), -jnp.inf, F32)

    def max_step(j, mask):
        red_ref[...] = jnp.maximum(red_ref[...], logits(j, mask))

    for d in range(r):
        max_step(i * r + d, band_masks[d])

    def max_body(t, carry):
        for d in range(r):
            max_step(t * r + d, None)
        return carry

    lax.fori_loop(0, i, max_body, 0)
    row_max = jnp.max(red_ref[...], axis=-1, keepdims=True)

    bias_ref[...] = jnp.broadcast_to(cq_ref[...] - row_max, (tq, tk))
    red_ref[...] = jnp.zeros((tq, tk), F32)
    acc_ref[...] = jnp.zeros_like(acc_ref)

    def prob_stage(j, mask, slot):
        p = jnp.exp(logits(j, mask))
        red_ref[...] += p
        p_ref[slot] = p.astype(BF16)

    def value_stage(j, slot):
        start = pl.multiple_of(j * tk, tk)
        acc_ref[...] += _dot(p_ref[slot], v_ref[pl.ds(start, tk), :])

    prev = None
    for d in range(r):
        prob_stage(i * r + d, band_masks[d], d)
        if prev is not None:
            value_stage(*prev)
        prev = (i * r + d, d)

    def pv_body(t, carry):
        base = t * r
        pending = (jnp.where(t == 0, (i + 1) * r - 1, base - 1), r - 1)
        for d in range(r):
            prob_stage(base + d, None, d)
            value_stage(*pending)
            pending = (base + d, d)
        return carry

    lax.fori_loop(0, i, pv_body, 0)
    value_stage(jnp.where(i == 0, r - 1, i * r - 1), r - 1)
    denom = jnp.sum(red_ref[...], axis=-1, keepdims=True)
    o_ref[...] = (acc_ref[...] / denom).astype(o_ref.dtype)


def _fox_attention(q, kv, cum_log_f, batch, seq, n_heads, head_dim, tq=ATTN_TQ, tk=ATTN_TK):
    tq, tk = min(tq, seq), min(tk, seq)
    nq = seq // tq
    r = tq // tk
    cq = cum_log_f.reshape(batch, n_heads, seq, 1)
    ck = cum_log_f.reshape(batch, n_heads, 1, seq)
    return pl.pallas_call(
        functools.partial(_fox_attn_kernel, tq=tq, tk=tk, scale=head_dim ** -0.5),
        grid=(batch, n_heads, nq),
        in_specs=[
            pl.BlockSpec((tq, head_dim), lambda b, h, i: (b * nq + i, h)),
            pl.BlockSpec((seq, head_dim), lambda b, h, i: (b, h)),
            pl.BlockSpec((seq, head_dim), lambda b, h, i: (b, n_heads + h)),
            pl.BlockSpec((None, None, tq, 1), lambda b, h, i: (b, h, i, 0)),
            pl.BlockSpec((None, None, 1, seq), lambda b, h, i: (b, h, 0, 0)),
        ],
        out_specs=pl.BlockSpec((tq, head_dim), lambda b, h, i: (b * nq + i, h)),
        out_shape=jax.ShapeDtypeStruct((batch * seq, n_heads * head_dim), BF16),
        scratch_shapes=[
            pltpu.VMEM((tq, head_dim), F32),
            pltpu.VMEM((tq, tk), F32),
            pltpu.VMEM((tq, tk), F32),
            pltpu.VMEM((r, tq, tk), BF16),
        ],
        compiler_params=_params("parallel", "parallel", "parallel"),
        name="fox_attention",
    )(q, kv, kv, cq, ck)


def _route_kernel(l_ref, w_ref, pos_ref, meta_ref, *, tm, chunk):
    l = l_ref[...]
    n_exp, t = l.shape
    e_idx = lax.broadcasted_iota(jnp.int32, l.shape, 0)
    m1 = jnp.max(l, axis=0, keepdims=True)
    i1 = jnp.min(jnp.where(l == m1, e_idx, n_exp), axis=0, keepdims=True)
    sel1 = e_idx == i1
    l2 = jnp.where(sel1, -jnp.inf, l)
    m2 = jnp.max(l2, axis=0, keepdims=True)
    i2 = jnp.min(jnp.where(l2 == m2, e_idx, n_exp), axis=0, keepdims=True)
    sel2 = e_idx == i2
    e2 = jnp.exp(m2 - m1)
    denom = 1.0 + e2
    w_ref[0:1, :] = 1.0 / denom
    w_ref[1:2, :] = e2 / denom

    s1 = jnp.where(sel1, 1.0, 0.0)
    s2 = jnp.where(sel2, 1.0, 0.0)
    s_any = (s1 + s2).astype(BF16)
    cnt_col = jnp.sum(s1 + s2, axis=1, keepdims=True)
    cnt_row = _dot_nt(jnp.ones((n_exp, t), BF16), s_any)

    def padded(c):
        return jnp.floor((c + (tm - 1)) * (1.0 / tm)) * tm

    ei = lax.broadcasted_iota(jnp.int32, (n_exp, n_exp), 0)
    ej = lax.broadcasted_iota(jnp.int32, (n_exp, n_exp), 1)
    start_col = jnp.sum(jnp.where(ej < ei, padded(cnt_row), 0.0), axis=1, keepdims=True)
    end_col = start_col + padded(cnt_col)

    jj = lax.broadcasted_iota(jnp.int32, (chunk, chunk), 0)
    tt = lax.broadcasted_iota(jnp.int32, (chunk, chunk), 1)
    before = jnp.where(jj < tt, 1.0, 0.0).astype(BF16)
    carry = jnp.zeros((n_exp, 1), F32)
    for c in range(t // chunk):
        sl = slice(c * chunk, (c + 1) * chunk)
        dest = start_col + carry + _dot(s_any[:, sl], before)
        pos_ref[0:1, sl] = jnp.sum(s1[:, sl] * dest, axis=0, keepdims=True).astype(jnp.int32)
        pos_ref[1:2, sl] = jnp.sum(s2[:, sl] * dest, axis=0, keepdims=True).astype(jnp.int32)
        carry = carry + jnp.sum(s1[:, sl] + s2[:, sl], axis=1, keepdims=True)

    lanes = meta_ref.shape[1]
    blk_start = lax.broadcasted_iota(jnp.int32, (n_exp, lanes), 1).astype(F32) * tm
    blk_expert = jnp.sum(jnp.where(end_col <= blk_start, 1.0, 0.0), axis=0, keepdims=True)
    blk_expert = jnp.minimum(blk_expert, n_exp - 1.0)
    n_used = jnp.max(end_col, axis=0, keepdims=True) * (1.0 / tm)
    in_group = (start_col <= blk_start) & (blk_start < end_col)
    blk_rows = jnp.sum(
        jnp.where(in_group, jnp.clip(start_col + cnt_col - blk_start, 0.0, tm), 0.0),
        axis=0, keepdims=True)
    meta_row = lax.broadcasted_iota(jnp.int32, meta_ref.shape, 0)
    meta = jnp.where(meta_row == 0, blk_expert, jnp.where(meta_row == 1, n_used, blk_rows))
    meta_ref[...] = meta.astype(jnp.int32)


def _route(logits_t, tm):
    n_exp, t = logits_t.shape
    assert tm & (tm - 1) == 0, "row tile must be a power of two"
    assert TOP_K * t // tm + n_exp <= VREG_LANES
    chunk = min(1024, t)
    return pl.pallas_call(
        functools.partial(_route_kernel, tm=tm, chunk=chunk),
        out_shape=[
            jax.ShapeDtypeStruct((TOP_K, t), F32),
            jax.ShapeDtypeStruct((TOP_K, t), jnp.int32),
            jax.ShapeDtypeStruct((VREG_SUBLANES, VREG_LANES), jnp.int32),
        ],
        compiler_params=pltpu.CompilerParams(vmem_limit_bytes=V7X_VMEM_LIMIT_BYTES),
        name="moe_route",
    )(logits_t)


def _dispatch_kernel(pos_ref, x_ref, init_hbm, o_hbm, sem, *, tb):
    del init_hbm

    def issue(r, carry):
        for c in range(TOP_K):
            pltpu.make_async_copy(x_ref.at[r], o_hbm.at[pos_ref[c, r]], sem).start()
        return carry

    lax.fori_loop(0, tb, issue, 0, unroll=DMA_ISSUE_UNROLL)
    for c in range(TOP_K):
        pltpu.make_async_copy(x_ref, o_hbm.at[pl.ds(0, tb)], sem).wait()


def _dispatch(pos, x, m_pad, tb=256):
    t, d = x.shape
    tb = min(tb, t)
    x3 = x.reshape(t, d // VREG_LANES, VREG_LANES)
    out = pl.pallas_call(
        functools.partial(_dispatch_kernel, tb=tb),
        grid=(t // tb,),
        in_specs=[
            pl.BlockSpec((TOP_K, tb), lambda i: (0, i), memory_space=pltpu.SMEM),
            pl.BlockSpec((tb,) + x3.shape[1:], lambda i: (i, 0, 0)),
            pl.BlockSpec(memory_space=pl.ANY),
        ],
        out_specs=pl.BlockSpec(memory_space=pl.ANY),
        out_shape=jax.ShapeDtypeStruct((m_pad,) + x3.shape[1:], x.dtype),
        scratch_shapes=[pltpu.SemaphoreType.DMA(())],
        input_output_aliases={2: 0},
        compiler_params=_params("arbitrary"),
        name="moe_dispatch",
    )(pos, x3, jnp.zeros((m_pad,) + x3.shape[1:], x.dtype))
    return out.reshape(m_pad, d)


META_EXPERT, META_N_USED, META_ROWS = 0, 1, 2


def _clamp_block(b, meta_ref):
    return jnp.minimum(b, meta_ref[META_N_USED, 0] - 1)


def _block_expert(b, meta_ref):
    return meta_ref[META_EXPERT, _clamp_block(b, meta_ref)]


def _row_branches(rows, tm, quantum, branch):
    pl.when(rows == 0)(lambda: branch(0))
    for n in range(quantum, tm + 1, quantum):
        pl.when((rows > n - quantum) & (rows <= n))(functools.partial(branch, n))


def _gmm_swiglu_kernel(meta_ref, a_ref, wg_ref, wu_ref, wd_ref, o_ref, wd_bf16_ref, *, tm,
                       quantum):
    rows = meta_ref[META_ROWS, pl.program_id(1)]

    def branch(n_rows):
        wd_bf16_ref[...] = wd_ref[...].astype(BF16)
        if n_rows:
            a = a_ref[:n_rows]
            g = _dot(a, wg_ref[...].astype(BF16))
            u = _dot(a, wu_ref[...].astype(BF16))
            o_ref[:n_rows] = (g * jax.nn.sigmoid(g) * u).astype(o_ref.dtype)
        if n_rows < tm:
            o_ref[n_rows:] = jnp.zeros((tm - n_rows,) + o_ref.shape[1:], o_ref.dtype)

    _row_branches(rows, tm, quantum, branch)


def _cast_chunk_rows(total_rows, n_steps):
    bf16_tile_rows = 2 * VREG_SUBLANES
    for rows in range(bf16_tile_rows, total_rows + 1, bf16_tile_rows):
        if total_rows % rows == 0 and total_rows // rows <= n_steps:
            return rows
    raise ValueError("no chunking of the down weights fits the grid")


def _grouped_swiglu(meta, a, w_gu, w_down, layer, tm, quantum, tn=1024):
    m_pad, k = a.shape
    f = w_gu.shape[-1] // 2
    tn = min(tn, f)
    nb = m_pad // tm
    rows, d = w_down.shape
    chunk = _cast_chunk_rows(rows, (f // tn) * nb)
    last_chunk = rows // chunk - 1

    def a_map(n, b, meta):
        return (_clamp_block(b, meta), 0)

    def w_map(off):
        return lambda n, b, meta: (layer, _block_expert(b, meta), 0, n + off)

    def chunk_map(n, b, meta):
        return (jnp.minimum(n * nb + b, last_chunk), 0)

    return pl.pallas_call(
        functools.partial(_gmm_swiglu_kernel, tm=tm, quantum=quantum),
        grid_spec=pltpu.PrefetchScalarGridSpec(
            num_scalar_prefetch=1,
            grid=(f // tn, nb),
            in_specs=[
                pl.BlockSpec((tm, k), a_map),
                pl.BlockSpec((None, None, k, tn), w_map(0)),
                pl.BlockSpec((None, None, k, tn), w_map(f // tn)),
                pl.BlockSpec((chunk, d), chunk_map),
            ],
            out_specs=[
                pl.BlockSpec((tm, tn), lambda n, b, meta: (b, n)),
                pl.BlockSpec((chunk, d), chunk_map),
            ],
        ),
        out_shape=[
            jax.ShapeDtypeStruct((m_pad, f), BF16),
            jax.ShapeDtypeStruct((rows, d), BF16),
        ],
        compiler_params=_params("arbitrary", "arbitrary"),
        name="moe_gu",
    )(meta, a, w_gu, w_gu, w_down)


def _gmm_down_kernel(meta_ref, a_ref, w_ref, o_ref, *, tm, quantum):
    rows = meta_ref[META_ROWS, pl.program_id(0)]

    @pl.when(pl.program_id(1) == 0)
    def _():
        o_ref[...] = jnp.zeros_like(o_ref)

    def branch(n_rows):
        if n_rows:
            o_ref[:n_rows] += _dot(a_ref[:n_rows], w_ref[...])

    _row_branches(rows, tm, quantum, branch)


def _grouped_down(meta, a, w_down_bf16, tm, quantum, tk=1792):
    m_pad, k_dim = a.shape
    n = w_down_bf16.shape[-1]
    tk = min(tk, k_dim)
    nk = k_dim // tk
    nb = m_pad // tm

    def k_eff(b, k, meta):
        return jnp.where(b < meta[META_N_USED, 0], k, nk - 1)

    return pl.pallas_call(
        functools.partial(_gmm_down_kernel, tm=tm, quantum=quantum),
        grid_spec=pltpu.PrefetchScalarGridSpec(
            num_scalar_prefetch=1,
            grid=(nb, nk),
            in_specs=[
                pl.BlockSpec((tm, tk),
                             lambda b, k, meta: (_clamp_block(b, meta), k_eff(b, k, meta))),
                pl.BlockSpec((tk, n),
                             lambda b, k, meta: (_block_expert(b, meta) * nk + k_eff(b, k, meta),
                                                 0)),
            ],
            out_specs=pl.BlockSpec((tm, n), lambda b, k, meta: (b, 0)),
        ),
        out_shape=jax.ShapeDtypeStruct((m_pad, n), F32),
        compiler_params=_params("arbitrary", "arbitrary"),
        name="moe_down",
    )(meta, a, w_down_bf16)


def _combine_kernel(pos_ref, pos_next_ref, y_hbm, h_ref, w_ref, g_ref, o_ref, buf_ref, sems,
                    *, tb, n_steps):
    i = pl.program_id(0)
    slot = lax.rem(i, 2)

    def gather(p_ref, s):
        def issue(r, carry):
            for c in range(TOP_K):
                pltpu.make_async_copy(y_hbm.at[pl.ds(p_ref[c, r], 1)],
                                      buf_ref.at[s, c, pl.ds(r, 1)], sems.at[s]).start()
            return carry

        lax.fori_loop(0, tb, issue, 0, unroll=DMA_ISSUE_UNROLL)

    @pl.when(i == 0)
    def _():
        gather(pos_ref, 0)

    @pl.when(i + 1 < n_steps)
    def _():
        gather(pos_next_ref, 1 - slot)

    for c in range(TOP_K):
        pltpu.make_async_copy(y_hbm.at[pl.ds(0, tb)], buf_ref.at[slot, c], sems.at[slot]).wait()
    w = w_ref[...]
    h = h_ref[...] + (w[:, 0:1] * buf_ref[slot, 0] + w[:, 1:2] * buf_ref[slot, 1])
    y = h * lax.rsqrt(jnp.mean(h * h, axis=-1, keepdims=True) + RMS_EPS)
    o_ref[...] = (y * g_ref[...]).astype(o_ref.dtype)


def _combine_norm(pos, y_sorted, h, gate_w, gain, out_dtype, tb=256):
    t, d = h.shape
    tb = min(tb, t)
    n_steps = t // tb
    return pl.pallas_call(
        functools.partial(_combine_kernel, tb=tb, n_steps=n_steps),
        grid=(n_steps,),
        in_specs=[
            pl.BlockSpec((TOP_K, tb), lambda i: (0, i), memory_space=pltpu.SMEM),
            pl.BlockSpec((TOP_K, tb), lambda i: (0, jnp.minimum(i + 1, n_steps - 1)),
                         memory_space=pltpu.SMEM),
            pl.BlockSpec(memory_space=pl.ANY),
            pl.BlockSpec((tb, d), lambda i: (i, 0)),
            pl.BlockSpec((tb, TOP_K), lambda i: (i, 0)),
            pl.BlockSpec((1, d), lambda i: (0, 0)),
        ],
        out_specs=pl.BlockSpec((tb, d), lambda i: (i, 0)),
        out_shape=jax.ShapeDtypeStruct((t, d), out_dtype),
        scratch_shapes=[pltpu.VMEM((2, TOP_K, tb, d), F32), pltpu.SemaphoreType.DMA((2,))],
        compiler_params=_params("arbitrary"),
        name="moe_combine_norm",
    )(pos, pos, y_sorted, h, gate_w, gain.reshape(1, d).astype(F32))


def kernel(x, norm_mix, norm_ffn, w_qkv_a, w_o_a, norm_kv, w_kvf_b, b_f, w_q_b, w_o_b,
           w_gu_dense, w_down_dense, w_router, w_gu_exp, w_down_exp, norm_final):
    batch, seq, d_model = x.shape
    n_heads = b_f.shape[0]
    d_attn = w_o_a.shape[1]
    head_dim = d_attn // n_heads
    n_experts = w_router.shape[-1]
    assert TOP_K == 2 and norm_mix.shape[0] == 2, "kernel is written for the depth-2 trunk"
    t = batch * seq
    h = x.reshape(t, d_model)

    (hn,) = _rmsnorm(h, [norm_mix[0]], BF16)
    qkv = _matmul(hn, w_qkv_a, (0,), 3 * d_attn, BF16, name="mm_qkv_a")
    o = _sb_attention(qkv, batch, seq, n_heads, head_dim)
    h = _matmul(o, w_o_a, (0,), d_model, F32, residual=h, name="mm_o_a")
    (hn,) = _rmsnorm(h, [norm_ffn[0]], BF16)
    act = _matmul_swiglu(hn, w_gu_dense, (0,), name="mm_gu_dense")
    h = _matmul_down(act, w_down_dense, (0,), h, name="mm_down_dense")

    w_f_t = w_kvf_b[:, 2 * d_attn:].T
    hn_q, hn_kv, flogit_t = _rmsnorm(h, [norm_mix[1], norm_kv], BF16, proj_t=w_f_t)
    kv = _matmul(hn_kv, w_kvf_b, (), 2 * d_attn, BF16, name="mm_kv_b")
    q = _matmul(hn_q, w_q_b, (0,), d_attn, BF16, name="mm_q_b")
    cum_log_f = _fgate_cumsum(flogit_t, b_f, batch, seq)
    o = _fox_attention(q, kv, cum_log_f, batch, seq, n_heads, head_dim)
    h = _matmul(o, w_o_b, (0,), d_model, F32, residual=h, name="mm_o_b")

    hn, logits_t = _rmsnorm(h, [norm_ffn[1]], BF16, proj_t=w_router[0].T)
    tm = min(MOE_ROW_TILE, t)
    m_pad = TOP_K * t + n_experts * tm
    gate_w, pos, meta = _route(logits_t, tm)
    x_sorted = _dispatch(pos, hn, m_pad)
    w_down = w_down_exp[0].reshape(-1, d_model)
    quantum = min(MOE_ROW_QUANTUM, tm)
    act, w_down_bf16 = _grouped_swiglu(meta, x_sorted, w_gu_exp, w_down, 0, tm, quantum)
    y_sorted = _grouped_down(meta, act, w_down_bf16, tm, quantum)
    out = _combine_norm(pos, y_sorted, h, gate_w.T, norm_final, x.dtype)
    return out.reshape(batch, seq, d_model)
```
